```python
import math
import jax, jax.numpy as jnp
from jax import lax
import numpy as np

D_MODEL = 1024
BATCH = 8
SEQ = 2048
DEPTH = 4
DEC_BATCH = 128
DEC_SEQ = 4
PAST_LEN = 16384
PAGE_SIZE = 128

N_MIXERS = 2
N_A = (DEPTH + 1) // 2
N_B = DEPTH // 2
H_A = 8
DV_A = D_MODEL // H_A
DK_A = DV_A // 2
QK_A = H_A * DK_A
VW_A = H_A * DV_A
A_IN = 2 * QK_A + 2 * VW_A + 2 * H_A
GATE_CAP = 15.0
HK_B = 8
HV_B = 16
DK_B = 128
DV_B = 128
QK_B = HK_B * DK_B
VW_B = HV_B * DV_B
CONV_W = 4
CONV_DIM = 2 * QK_B + VW_B
B_IN = CONV_DIM + VW_B + 2 * HV_B
CHUNK = 64
D_FF = 4 * D_MODEL
ALPHA = (2.0 * DEPTH) ** 0.25
BETA_INIT = (8.0 * DEPTH) ** -0.25
LN_EPS = 1e-5
RMS_EPS = 1e-6

kernel_name = "hybrid_mlstm_gdn_decode_step"


def layer_norm(x, g, b):
    xf = x.astype(jnp.float32)
    mu = xf.mean(-1, keepdims=True)
    var = jnp.square(xf - mu).mean(-1, keepdims=True)
    return ((xf - mu) * lax.rsqrt(var + LN_EPS) * g.astype(jnp.float32) + b.astype(jnp.float32)).astype(x.dtype)


def head_rms(h):
    return h * lax.rsqrt(jnp.mean(jnp.square(h), -1, keepdims=True) + RMS_EPS)


def l2norm(x):
    return x * lax.rsqrt(jnp.sum(jnp.square(x), -1, keepdims=True) + RMS_EPS)


def to_chunks(a, L):
    B, T = a.shape[:2]
    a = a.reshape((B, T // L, L) + a.shape[2:])
    return jnp.swapaxes(jnp.moveaxis(a, 1, 0), 2, 3)


def from_chunks(a):
    a = jnp.swapaxes(jnp.moveaxis(a, 0, 1), 2, 3)
    return a.reshape((a.shape[0], a.shape[1] * a.shape[2]) + a.shape[3:])


def mlstm_chunked(q, k, v, i_pre, logf, C0, n0, m0):
    T = q.shape[1]
    L = math.gcd(T, CHUNK)
    causal = jnp.tril(jnp.ones((L, L), dtype=bool))
    xs = (to_chunks(q, L), to_chunks(k, L), to_chunks(v, L), to_chunks(i_pre, L), to_chunks(logf, L))

    def step(carry, xc):
        C, n, m = carry
        qc, kc, vc, ic, fc = xc
        b = jnp.cumsum(fc, axis=-1)
        D = jnp.where(causal, b[..., :, None] - b[..., None, :] + ic[..., None, :], -jnp.inf)
        inter = b + m[..., None]
        m_t = jnp.maximum(inter, D.max(-1))
        w_intra = jnp.exp(D - m_t[..., None])
        w_inter = jnp.exp(inter - m_t)
        s = jnp.einsum('bhtk,bhsk->bhts', qc, kc) * w_intra
        num = jnp.einsum('bhts,bhsv->bhtv', s, vc) + w_inter[..., None] * jnp.einsum('bhtk,bhkv->bhtv', qc, C)
        den = s.sum(-1) + w_inter * jnp.einsum('bhtk,bhk->bht', qc, n)
        h = num / jnp.maximum(jnp.abs(den), jnp.exp(-m_t))[..., None]
        m_new = m_t[..., -1]
        w_k = jnp.exp(b[..., -1:] - b + ic - m_new[..., None])
        decay = jnp.exp(b[..., -1] + m - m_new)
        C_new = decay[..., None, None] * C + jnp.einsum('bhs,bhsk,bhsv->bhkv', w_k, kc, vc)
        n_new = decay[..., None] * n + jnp.einsum('bhs,bhsk->bhk', w_k, kc)
        return (C_new, n_new, m_new), h

    (C, n, m), h = lax.scan(step, (C0, n0, m0), xs)
    return from_chunks(h), C, n, m


def gated_delta_chunked(q, k, v, g, beta, S0):
    T = q.shape[1]
    L = math.gcd(T, CHUNK)
    qc, kc, vc, gc, bc = (to_chunks(a, L) for a in (q, k, v, g, beta))
    G = jnp.cumsum(gc, axis=-1)
    tril = jnp.tril(jnp.ones((L, L), dtype=bool))
    strict = jnp.tril(jnp.ones((L, L), dtype=bool), -1)
    decay = jnp.exp(jnp.where(tril, G[..., :, None] - G[..., None, :], -jnp.inf))
    kb = kc * bc[..., None]
    M = jnp.where(strict, jnp.einsum('...tk,...sk->...ts', kb, kc) * decay, 0.0)
    A = M + jnp.eye(L, dtype=M.dtype)
    rhs = jnp.concatenate([vc * bc[..., None], kb * jnp.exp(G)[..., None]], axis=-1)
    sol = lax.linalg.triangular_solve(A, rhs, left_side=True, lower=True)
    u, w = sol[..., :DV_B], sol[..., DV_B:]
    attn = jnp.einsum('...tk,...sk->...ts', qc, kc) * decay
    qg = qc * jnp.exp(G)[..., None]
    kdec = kc * jnp.exp(G[..., -1:] - G)[..., None]
    gL = jnp.exp(G[..., -1])

    def step(S, xc):
        u_c, w_c, attn_c, qg_c, kdec_c, gL_c = xc
        v_new = u_c - jnp.einsum('bhtk,bhkv->bhtv', w_c, S)
        o = jnp.einsum('bhtk,bhkv->bhtv', qg_c, S) + jnp.einsum('bhts,bhsv->bhtv', attn_c, v_new)
        S = gL_c[..., None, None] * S + jnp.einsum('bhsk,bhsv->bhkv', kdec_c, v_new)
        return S, o

    S, o = lax.scan(step, S0, (u, w, attn, qg, kdec, gL))
    return from_chunks(o), S


def mlstm_mixer(x, w_in, gate_b, norm_w, w_out, C0, n0, m0):
    B, T, _ = x.shape
    f32 = jnp.float32
    proj = x @ w_in
    q = proj[..., :QK_A].reshape(B, T, H_A, DK_A).astype(f32)
    k = proj[..., QK_A:2 * QK_A].reshape(B, T, H_A, DK_A).astype(f32) * (DK_A ** -0.5)
    v = proj[..., 2 * QK_A:2 * QK_A + VW_A].reshape(B, T, H_A, DV_A).astype(f32)
    o = proj[..., 2 * QK_A + VW_A:2 * QK_A + 2 * VW_A].astype(f32)
    gates = proj[..., 2 * QK_A + 2 * VW_A:].astype(f32) + gate_b.astype(f32)
    gates = GATE_CAP * jnp.tanh(gates / GATE_CAP)
    i_pre = gates[..., :H_A]
    logf = jax.nn.log_sigmoid(gates[..., H_A:])
    h, C, n, m = mlstm_chunked(q, k, v, i_pre, logf, C0.astype(f32), n0.astype(f32), m0.astype(f32))
    h = head_rms(h) * norm_w.astype(f32).reshape(H_A, DV_A)
    h = jax.nn.sigmoid(o) * h.reshape(B, T, VW_A)
    return h.astype(x.dtype) @ w_out, C, n, m


def gdn_mixer(x, w_in, conv_w, dt_bias, a_log, norm_w, w_out, S0, conv0):
    B, T, _ = x.shape
    f32 = jnp.float32
    proj = x @ w_in
    qkv = proj[..., :CONV_DIM]
    z = proj[..., CONV_DIM:CONV_DIM + VW_B].astype(f32)
    b = proj[..., CONV_DIM + VW_B:CONV_DIM + VW_B + HV_B].astype(f32)
    a = proj[..., CONV_DIM + VW_B + HV_B:].astype(f32)
    xp = jnp.concatenate([conv0.astype(qkv.dtype), qkv], axis=1)
    c = xp[:, 0:T] * conv_w[0]
    for j in range(1, CONV_W):
        c = c + xp[:, j:j + T] * conv_w[j]
    c = jax.nn.silu(c.astype(f32))
    new_conv = xp[:, T:]
    rep = HV_B // HK_B
    q = l2norm(c[..., :QK_B].reshape(B, T, HK_B, DK_B)) * (DK_B ** -0.5)
    k = l2norm(c[..., QK_B:2 * QK_B].reshape(B, T, HK_B, DK_B))
    v = c[..., 2 * QK_B:].reshape(B, T, HV_B, DV_B)
    q = jnp.repeat(q, rep, axis=2)
    k = jnp.repeat(k, rep, axis=2)
    beta = jax.nn.sigmoid(b)
    g = -jnp.exp(a_log.astype(f32)) * jax.nn.softplus(a + dt_bias.astype(f32))
    o, S = gated_delta_chunked(q, k, v, g, beta, S0.astype(f32))
    o = head_rms(o) * norm_w.astype(f32).reshape(HV_B, DV_B)
    o = o.reshape(B, T, VW_B) * jax.nn.silu(z)
    return o.astype(x.dtype) @ w_out, S, new_conv


def sq_relu_mlp(x, w1, w2):
    return jnp.square(jax.nn.relu(x @ w1)) @ w2


def trunk(x, C, n, m, S, conv, a_w_in, a_gate_b, a_norm_w, a_w_out, b_w_in, b_conv_w, b_dt_bias,
          b_a_log, b_norm_w, b_w_out, mlp_w1, mlp_w2, ln1_g, ln1_b, ln2_g, ln2_b):
    new_C, new_n, new_m, new_S, new_conv = [], [], [], [], []
    for layer in range(DEPTH):
        j = layer // N_MIXERS
        if layer % N_MIXERS == 0:
            y, Cj, nj, mj = mlstm_mixer(x, a_w_in[j], a_gate_b[j], a_norm_w[j], a_w_out[j], C[j], n[j], m[j])
            new_C.append(Cj.astype(C.dtype)); new_n.append(nj.astype(n.dtype)); new_m.append(mj.astype(m.dtype))
        else:
            y, Sj, cj = gdn_mixer(x, b_w_in[j], b_conv_w[j], b_dt_bias[j], b_a_log[j], b_norm_w[j], b_w_out[j],
                                  S[j], conv[j])
            new_S.append(Sj.astype(S.dtype)); new_conv.append(cj.astype(conv.dtype))
        x = layer_norm(ALPHA * x + y, ln1_g[layer], ln1_b[layer])
        x = layer_norm(ALPHA * x + sq_relu_mlp(x, mlp_w1[layer], mlp_w2[layer]), ln2_g[layer], ln2_b[layer])
    return x, jnp.stack(new_C), jnp.stack(new_n), jnp.stack(new_m), jnp.stack(new_S), jnp.stack(new_conv)


def setup_inputs(seed: int = 0) -> dict:
    key = jax.random.key(seed)
    ks = jax.random.split(key, 24)
    nrm = jax.random.normal
    f32 = jnp.float32
    x_prompt = nrm(ks[0], (BATCH, SEQ, D_MODEL), f32)
    x_sample = nrm(ks[1], (DEC_BATCH, DEC_SEQ, D_MODEL), f32)
    state_mlstm_C = nrm(ks[2], (N_A, DEC_BATCH, H_A, DK_A, DV_A), f32)
    state_mlstm_n = nrm(ks[3], (N_A, DEC_BATCH, H_A, DK_A), f32)
    state_mlstm_m = nrm(ks[4], (N_A, DEC_BATCH, H_A), f32)
    state_gdn_S = nrm(ks[5], (N_B, DEC_BATCH, HV_B, DK_B, DV_B), f32)
    state_gdn_conv = nrm(ks[6], (N_B, DEC_BATCH, CONV_W - 1, CONV_DIM), f32)
    a_scale = jnp.concatenate([jnp.ones((2 * QK_A,), f32), jnp.full((VW_A,), BETA_INIT, f32),
                               jnp.ones((VW_A + 2 * H_A,), f32)])
    a_w_in = nrm(ks[7], (N_A, D_MODEL, A_IN), f32) * (D_MODEL ** -0.5) * a_scale
    i_bias = 0.1 * nrm(ks[8], (N_A, H_A), f32)
    f_bias = jnp.linspace(3.0, 6.0, H_A, dtype=f32)[None, :] + 0.1 * nrm(ks[9], (N_A, H_A), f32)
    a_gate_b = jnp.concatenate([i_bias, f_bias], axis=-1)
    a_norm_w = 1.0 + 0.05 * nrm(ks[10], (N_A, VW_A), f32)
    a_w_out = nrm(ks[11], (N_A, VW_A, D_MODEL), f32) * (VW_A ** -0.5) * BETA_INIT
    b_scale = jnp.concatenate([jnp.ones((2 * QK_B,), f32), jnp.full((VW_B,), BETA_INIT, f32),
                               jnp.ones((VW_B + 2 * HV_B,), f32)])
    b_w_in = nrm(ks[12], (N_B, D_MODEL, B_IN), f32) * (D_MODEL ** -0.5) * b_scale
    b_conv_w = nrm(ks[13], (N_B, CONV_W, CONV_DIM), f32) * (CONV_W ** -0.5)
    dt = jnp.exp(jax.random.uniform(ks[14], (N_B, HV_B), f32, math.log(1e-3), math.log(1e-1)))
    b_dt_bias = dt + jnp.log(-jnp.expm1(-dt))
    b_a_log = jnp.log(jax.random.uniform(ks[15], (N_B, HV_B), f32, 1.0, 16.0))
    b_norm_w = 1.0 + 0.05 * nrm(ks[16], (N_B, VW_B), f32)
    b_w_out = nrm(ks[17], (N_B, VW_B, D_MODEL), f32) * (VW_B ** -0.5) * BETA_INIT
    mlp_w1 = nrm(ks[18], (DEPTH, D_MODEL, D_FF), f32) * (D_MODEL ** -0.5) * BETA_INIT
    mlp_w2 = nrm(ks[19], (DEPTH, D_FF, D_MODEL), f32) * (D_FF ** -0.5) * BETA_INIT
    ln1_g = 1.0 + 0.05 * nrm(ks[20], (DEPTH, D_MODEL), f32)
    ln1_b = 0.02 * nrm(ks[21], (DEPTH, D_MODEL), f32)
    ln2_g = 1.0 + 0.05 * nrm(ks[22], (DEPTH, D_MODEL), f32)
    ln2_b = 0.02 * nrm(ks[23], (DEPTH, D_MODEL), f32)
    return {"x_prompt": x_prompt, "x_sample": x_sample,
            "state_mlstm_C": state_mlstm_C, "state_mlstm_n": state_mlstm_n, "state_mlstm_m": state_mlstm_m,
            "state_gdn_S": state_gdn_S, "state_gdn_conv": state_gdn_conv,
            "a_w_in": a_w_in, "a_gate_b": a_gate_b, "a_norm_w": a_norm_w, "a_w_out": a_w_out,
            "b_w_in": b_w_in, "b_conv_w": b_conv_w, "b_dt_bias": b_dt_bias, "b_a_log": b_a_log,
            "b_norm_w": b_norm_w, "b_w_out": b_w_out,
            "mlp_w1": mlp_w1, "mlp_w2": mlp_w2,
            "ln1_g": ln1_g, "ln1_b": ln1_b, "ln2_g": ln2_g, "ln2_b": ln2_b}


def reference(x_prompt, x_sample, state_mlstm_C, state_mlstm_n, state_mlstm_m, state_gdn_S, state_gdn_conv,
              a_w_in, a_gate_b, a_norm_w, a_w_out, b_w_in, b_conv_w, b_dt_bias, b_a_log, b_norm_w, b_w_out,
              mlp_w1, mlp_w2, ln1_g, ln1_b, ln2_g, ln2_b):
    weights = (a_w_in, a_gate_b, a_norm_w, a_w_out, b_w_in, b_conv_w, b_dt_bias, b_a_log, b_norm_w, b_w_out,
               mlp_w1, mlp_w2, ln1_g, ln1_b, ln2_g, ln2_b)
    Bp = x_prompt.shape[0]
    dt = x_prompt.dtype
    C0 = jnp.zeros((N_A, Bp, H_A, DK_A, DV_A), dt)
    n0 = jnp.zeros((N_A, Bp, H_A, DK_A), dt)
    m0 = jnp.zeros((N_A, Bp, H_A), dt)
    S0 = jnp.zeros((N_B, Bp, HV_B, DK_B, DV_B), dt)
    conv0 = jnp.zeros((N_B, Bp, CONV_W - 1, CONV_DIM), dt)
    y_prompt, p_C, p_n, p_m, p_S, p_conv = trunk(x_prompt, C0, n0, m0, S0, conv0, *weights)
    y_sample, s_C, s_n, s_m, s_S, s_conv = trunk(x_sample, state_mlstm_C, state_mlstm_n, state_mlstm_m,
                                                 state_gdn_S, state_gdn_conv, *weights)
    return (y_prompt, y_sample, p_C, p_n, p_m, p_S, p_conv, s_C, s_n, s_m, s_S, s_conv)
```

```python
import functools

import jax
import jax.numpy as jnp
from jax import lax
from jax.experimental import pallas as pl
from jax.experimental.pallas import tpu as pltpu

F32 = jnp.float32
BF16 = jnp.bfloat16

D_MODEL = 1024
DEPTH = 4
H_A = 8
DV_A = 128
DK_A = 64
QK_A = H_A * DK_A
VW_A = H_A * DV_A
GATE_CAP = 15.0
HK_B = 8
HV_B = 16
DK_B = 128
DV_B = 128
QK_B = HK_B * DK_B
VW_B = HV_B * DV_B
CONV_W = 4
CONV_DIM = 2 * QK_B + VW_B
CHUNK = 64
D_FF = 4 * D_MODEL
ALPHA = (2.0 * DEPTH) ** 0.25
LN_EPS = 1e-5
RMS_EPS = 1e-6

LANES = 128
SUBLANES = 8
VMEM_LIMIT = 56 * 1024 * 1024
ROW_TILE = 512
FF_TILE = 1024
MLSTM_ROWS = 512
GDN_ROWS = 256

NEG_INF = float("-inf")


def _params(*sem):
    return pltpu.CompilerParams(dimension_semantics=sem, vmem_limit_bytes=VMEM_LIMIT)


def _dot(a, b):
    return jnp.dot(a, b, preferred_element_type=F32)


def _dot_nt(a, b):
    return lax.dot_general(a, b, (((1,), (1,)), ((), ())), preferred_element_type=F32)


def _dot_tn(a, b):
    return lax.dot_general(a, b, (((0,), (0,)), ((), ())), preferred_element_type=F32)


def _split2(x):
    hi = x.astype(BF16)
    lo = (x - hi.astype(F32)).astype(BF16)
    return hi, lo


def _dot_split(a, b):
    ah, al = _split2(a)
    bh, bl = _split2(b)
    return _dot(ah, bh) + (_dot(ah, bl) + _dot(al, bh))


def _cumsum_rows(tri_bf, x):
    x1 = x.astype(BF16)
    r1 = x - x1.astype(F32)
    x2 = r1.astype(BF16)
    x3 = (r1 - x2.astype(F32)).astype(BF16)
    return _dot(tri_bf, x1) + (_dot(tri_bf, x2) + _dot(tri_bf, x3))


def _layer_norm(r, g, b):
    mu = jnp.mean(r, axis=-1, keepdims=True)
    d = r - mu
    var = jnp.mean(d * d, axis=-1, keepdims=True)
    return d * lax.rsqrt(var + LN_EPS) * g + b


def _rms(h):
    return h * lax.rsqrt(jnp.mean(h * h, axis=-1, keepdims=True) + RMS_EPS)


def _l2norm(x):
    return x * lax.rsqrt(jnp.sum(x * x, axis=-1, keepdims=True) + RMS_EPS)


def _sigmoid(x):
    return jax.nn.sigmoid(x)


def _proj_kernel(x_ref, w_ref, *out_refs, segs):
    xb = x_ref[...].astype(BF16)
    for (start, width), o_ref in zip(segs, out_refs):
        for c0 in range(0, width, FF_TILE):
            cw = min(FF_TILE, width - c0)
            y = _dot(xb, w_ref[:, start + c0:start + c0 + cw])
            o_ref[:, c0:c0 + cw] = y.astype(o_ref.dtype)


def _proj(x, w, segs, dtypes):
    n, k = x.shape
    grid = (n // ROW_TILE,)
    return pl.pallas_call(
        functools.partial(_proj_kernel, segs=segs),
        grid=grid,
        in_specs=[pl.BlockSpec((ROW_TILE, k), lambda i: (i, 0)),
                  pl.BlockSpec(w.shape, lambda i: (0, 0))],
        out_specs=[pl.BlockSpec((ROW_TILE, wd), lambda i: (i, 0)) for _, wd in segs],
        out_shape=[jax.ShapeDtypeStruct((n, wd), dt) for (_, wd), dt in zip(segs, dtypes)],
        compiler_params=_params("parallel"),
        name="proj",
    )(x, w)


def _outproj_ln_kernel(h_ref, x_ref, w_ref, g_ref, b_ref, o_ref):
    y = _dot(h_ref[...], w_ref[...])
    o_ref[...] = _layer_norm(ALPHA * x_ref[...] + y, g_ref[...], b_ref[...])


def _outproj_ln(h, x, w, g, b):
    n, kh = h.shape
    d = x.shape[1]
    return pl.pallas_call(
        _outproj_ln_kernel,
        grid=(n // ROW_TILE,),
        in_specs=[pl.BlockSpec((ROW_TILE, kh), lambda i: (i, 0)),
                  pl.BlockSpec((ROW_TILE, d), lambda i: (i, 0)),
                  pl.BlockSpec((kh, d), lambda i: (0, 0)),
                  pl.BlockSpec((1, d), lambda i: (0, 0)),
                  pl.BlockSpec((1, d), lambda i: (0, 0))],
        out_specs=pl.BlockSpec((ROW_TILE, d), lambda i: (i, 0)),
        out_shape=jax.ShapeDtypeStruct((n, d), F32),
        compiler_params=_params("parallel"),
        name="outproj_ln",
    )(h, x, w, g, b)


def _mlp_ln_kernel(x_ref, w1_ref, w2_ref, g_ref, b_ref, o_ref, acc_ref):
    x = x_ref[...]
    xb = x.astype(BF16)
    dff = w1_ref.shape[1]
    for c0 in range(0, dff, FF_TILE):
        h = _dot(xb, w1_ref[:, c0:c0 + FF_TILE])
        h = jnp.square(jnp.maximum(h, 0.0)).astype(BF16)
        y = _dot(h, w2_ref[c0:c0 + FF_TILE, :])
        if c0 == 0:
            acc_ref[...] = y
        else:
            acc_ref[...] += y
    o_ref[...] = _layer_norm(ALPHA * x + acc_ref[...], g_ref[...], b_ref[...])


def _mlp_ln(x, w1, w2, g, b):
    n, d = x.shape
    dff = w1.shape[1]
    return pl.pallas_call(
        _mlp_ln_kernel,
        grid=(n // ROW_TILE,),
        in_specs=[pl.BlockSpec((ROW_TILE, d), lambda i: (i, 0)),
                  pl.BlockSpec((d, dff), lambda i: (0, 0)),
                  pl.BlockSpec((dff, d), lambda i: (0, 0)),
                  pl.BlockSpec((1, d), lambda i: (0, 0)),
                  pl.BlockSpec((1, d), lambda i: (0, 0))],
        out_specs=pl.BlockSpec((ROW_TILE, d), lambda i: (i, 0)),
        out_shape=jax.ShapeDtypeStruct((n, d), F32),
        scratch_shapes=[pltpu.VMEM((ROW_TILE, d), F32)],
        compiler_params=_params("parallel"),
        name="mlp_ln",
    )(x, w1, w2, g, b)


def _tri_masks(n):
    row = lax.broadcasted_iota(jnp.int32, (n, n), 0)
    col = lax.broadcasted_iota(jnp.int32, (n, n), 1)
    return row >= col, row > col, row == col


def _mlstm_gates(g_raw, gb):
    cap = GATE_CAP * jnp.tanh((g_raw + gb) / GATE_CAP)
    return cap, jax.nn.log_sigmoid(cap)


def _mlstm_prompt_kernel(q_ref, k_ref, v_ref, o_ref, g_ref, gb_ref, nw_ref,
                         hg_ref, c_out, n_out, m_out, c_s, n_s, m_s):
    t = pl.program_id(1)

    @pl.when(t == 0)
    def _():
        c_s[...] = jnp.zeros_like(c_s)
        n_s[...] = jnp.zeros_like(n_s)
        m_s[...] = jnp.zeros_like(m_s)

    causal, _, _ = _tri_masks(CHUNK)
    tri_bf = causal.astype(BF16)
    scale = DK_A ** -0.5

    def chunk(c, carry):
        r0 = pl.multiple_of(c * CHUNK, CHUNK)
        rows = pl.ds(r0, CHUNK)
        cap, lf = _mlstm_gates(g_ref[rows, :], gb_ref[...])
        bcum = _cumsum_rows(tri_bf, lf)
        cap_t = cap.T
        bcum_t = bcum.T
        for h in range(H_A):
            qh = q_ref[rows, h * DK_A:(h + 1) * DK_A]
            kh = k_ref[rows, h * DK_A:(h + 1) * DK_A]
            vh = v_ref[rows, h * DV_A:(h + 1) * DV_A]
            b_col = bcum[:, H_A + h:H_A + h + 1]
            b_row = bcum_t[H_A + h:H_A + h + 1, :]
            i_col = cap[:, h:h + 1]
            i_row = cap_t[h:h + 1, :]
            m_prev = m_s[:, h:h + 1]
            dmat = jnp.where(causal, b_col - b_row + i_row, NEG_INF)
            inter = b_col + m_prev
            m_t = jnp.maximum(inter, jnp.max(dmat, axis=-1, keepdims=True))
            w_intra = jnp.exp(dmat - m_t)
            w_inter = jnp.exp(inter - m_t)
            s = _dot_nt(qh, kh) * scale * w_intra
            c_h = c_s[h]
            n_row = n_s[h:h + 1, :]
            num = _dot(s.astype(BF16), vh) + w_inter * _dot(qh, c_h.astype(BF16))
            den = (jnp.sum(s, axis=-1, keepdims=True)
                   + w_inter * jnp.sum(qh.astype(F32) * n_row, axis=-1, keepdims=True))
            hh = num / jnp.maximum(jnp.abs(den), jnp.exp(-m_t))
            m_new = m_t[CHUNK - 1:CHUNK, :]
            b_last = b_col[CHUNK - 1:CHUNK, :]
            w_k = jnp.exp(b_last - b_col + i_col - m_new)
            decay = jnp.exp(b_last + m_prev - m_new)
            kw = kh.astype(F32) * scale * w_k
            c_s[h] = decay * c_h + _dot_tn(kw.astype(BF16), vh)
            n_s[h:h + 1, :] = decay * n_row + jnp.sum(kw, axis=0, keepdims=True)
            m_s[:, h:h + 1] = m_new
            cols = slice(h * DV_A, (h + 1) * DV_A)
            gate = _sigmoid(o_ref[rows, cols].astype(F32))
            hg_ref[rows, cols] = (gate * (_rms(hh) * nw_ref[:, cols])).astype(hg_ref.dtype)
        return carry

    lax.fori_loop(0, q_ref.shape[0] // CHUNK, chunk, 0)

    @pl.when(t == pl.num_programs(1) - 1)
    def _():
        c_out[0] = c_s[...]
        n_out[0] = n_s[...]
        m_out[0] = m_s[...]


def _mlstm_prompt(q, k, v, o, g, gb, nw, batch, seq):
    tb = min(MLSTM_ROWS, seq)
    nt = seq // tb
    row = lambda b, t: (b * nt + t, 0)
    const = lambda b, t: (0, 0)
    return pl.pallas_call(
        _mlstm_prompt_kernel,
        grid=(batch, nt),
        in_specs=[pl.BlockSpec((tb, QK_A), row), pl.BlockSpec((tb, QK_A), row),
                  pl.BlockSpec((tb, VW_A), row), pl.BlockSpec((tb, VW_A), row),
                  pl.BlockSpec((tb, LANES), row),
                  pl.BlockSpec((1, LANES), const), pl.BlockSpec((1, VW_A), const)],
        out_specs=[pl.BlockSpec((tb, VW_A), row),
                   pl.BlockSpec((1, H_A, DK_A, DV_A), lambda b, t: (b, 0, 0, 0)),
                   pl.BlockSpec((1, H_A, DK_A), lambda b, t: (b, 0, 0)),
                   pl.BlockSpec((1, 1, LANES), lambda b, t: (b, 0, 0))],
        out_shape=[jax.ShapeDtypeStruct((batch * seq, VW_A), BF16),
                   jax.ShapeDtypeStruct((batch, H_A, DK_A, DV_A), F32),
                   jax.ShapeDtypeStruct((batch, H_A, DK_A), F32),
                   jax.ShapeDtypeStruct((batch, 1, LANES), F32)],
        scratch_shapes=[pltpu.VMEM((H_A, DK_A, DV_A), F32),
                        pltpu.VMEM((H_A, DK_A), F32),
                        pltpu.VMEM((1, LANES), F32)],
        compiler_params=_params("parallel", "arbitrary"),
        name="mlstm_prompt",
    )(q, k, v, o, g, gb, nw)


def _mlstm_sample_kernel(kq_ref, v_ref, o_ref, g_ref, c0_ref, n0_ref, m0_ref, gb_ref, nw_ref,
                         hg_ref, c_out, n_out, m_out, h_s, *, steps):
    cap, lf = _mlstm_gates(g_ref[0], gb_ref[...])
    kq = kq_ref[0]
    v = v_ref[0]
    scale = DK_A ** -0.5
    heads_per_tile = LANES // DK_A
    for p in range(H_A // heads_per_tile):
        kq_t = kq[:, p * LANES:(p + 1) * LANES].T
        for j in range(heads_per_tile):
            h = p * heads_per_tile + j
            c = c0_ref[0, h]
            n_row = n0_ref[0, h:h + 1, :]
            m = m0_ref[0, :, h:h + 1]
            for t in range(steps):
                i_t = cap[t:t + 1, h:h + 1]
                lf_t = lf[t:t + 1, H_A + h:H_A + h + 1]
                m_new = jnp.maximum(lf_t + m, i_t)
                fp = jnp.exp(lf_t + m - m_new)
                ip = jnp.exp(i_t - m_new)
                k_col = kq_t[j * DK_A:(j + 1) * DK_A, t:t + 1] * scale
                q_col = kq_t[j * DK_A:(j + 1) * DK_A, steps + t:steps + t + 1]
                k_row = kq[t:t + 1, h * DK_A:(h + 1) * DK_A] * scale
                q_row = kq[steps + t:steps + t + 1, h * DK_A:(h + 1) * DK_A]
                v_row = v[t:t + 1, h * DV_A:(h + 1) * DV_A]
                c = fp * c + (ip * k_col) * v_row
                n_row = fp * n_row + ip * k_row
                num = jnp.sum(c * q_col, axis=0, keepdims=True)
                den = jnp.sum(n_row * q_row, axis=-1, keepdims=True)
                h_s[t:t + 1, h * DV_A:(h + 1) * DV_A] = num / jnp.maximum(jnp.abs(den), jnp.exp(-m_new))
                m = m_new
            c_out[0, h] = c
            n_out[0, h:h + 1, :] = n_row
            m_out[0, :, h:h + 1] = m
    for h in range(H_A):
        cols = slice(h * DV_A, (h + 1) * DV_A)
        hh = h_s[0:steps, cols]
        hg_ref[0, :, cols] = _sigmoid(o_ref[0, :, cols]) * (_rms(hh) * nw_ref[:, cols])


def _mlstm_sample(kq, v, o, g, c0, n0, m0, gb, nw):
    batch, steps = v.shape[0], v.shape[1]
    b3 = lambda b: (b, 0, 0)
    const = lambda b: (0, 0)
    return pl.pallas_call(
        functools.partial(_mlstm_sample_kernel, steps=steps),
        grid=(batch,),
        in_specs=[pl.BlockSpec((1, 2 * steps, QK_A), b3),
                  pl.BlockSpec((1, steps, VW_A), b3), pl.BlockSpec((1, steps, VW_A), b3),
                  pl.BlockSpec((1, steps, LANES), b3),
                  pl.BlockSpec((1, H_A, DK_A, DV_A), lambda b: (b, 0, 0, 0)),
                  pl.BlockSpec((1, H_A, DK_A), b3),
                  pl.BlockSpec((1, 1, H_A), b3),
                  pl.BlockSpec((1, LANES), const), pl.BlockSpec((1, VW_A), const)],
        out_specs=[pl.BlockSpec((1, steps, VW_A), b3),
                   pl.BlockSpec((1, H_A, DK_A, DV_A), lambda b: (b, 0, 0, 0)),
                   pl.BlockSpec((1, H_A, DK_A), b3),
                   pl.BlockSpec((1, 1, H_A), b3)],
        out_shape=[jax.ShapeDtypeStruct((batch, steps, VW_A), F32),
                   jax.ShapeDtypeStruct((batch, H_A, DK_A, DV_A), F32),
                   jax.ShapeDtypeStruct((batch, H_A, DK_A), F32),
                   jax.ShapeDtypeStruct((batch, 1, H_A), F32)],
        scratch_shapes=[pltpu.VMEM((SUBLANES, VW_A), F32)],
        compiler_params=_params("parallel"),
        name="mlstm_sample",
    )(kq, v, o, g, c0, n0, m0, gb, nw)


def _gdn_gates(ba, alog, dtb):
    beta = _sigmoid(ba)
    g = -jnp.exp(alog) * jax.nn.softplus(ba + dtb)
    return beta, g


def _unit_lower_inverse(nmat):
    size = nmat.shape[0]
    _, _, eye = _tri_masks(size)
    r = jnp.where(eye, 1.0, nmat)
    p = nmat
    span = 2
    while span < size:
        p = _dot_split(p, p)
        r = r + _dot_split(r, p)
        span *= 2
    return r


def _gdn_prompt_kernel(qkv_ref, z_ref, ba_ref, cw_ref, dtb_ref, alog_ref, nw_ref,
                       o_ref, s_out, conv_out, xpad_s, s_s):
    t = pl.program_id(1)
    tb = qkv_ref.shape[0]
    pad = SUBLANES

    @pl.when(t == 0)
    def _():
        s_s[...] = jnp.zeros_like(s_s)
        xpad_s[0:pad, :] = jnp.zeros((pad, CONV_DIM), F32)

    xpad_s[pad:pad + tb, :] = qkv_ref[...].astype(F32)

    tril, strict, _ = _tri_masks(CHUNK)
    tri_bf = tril.astype(BF16)
    scale = DK_B ** -0.5
    rep = HV_B // HK_B

    def chunk(c, carry):
        r0 = pl.multiple_of(c * CHUNK, CHUNK)
        rows = pl.ds(r0, CHUNK)

        def conv_silu(col0):
            y = xpad_s[pl.ds(r0, CHUNK + pad), col0:col0 + LANES]
            cw = cw_ref[:, col0:col0 + LANES]
            acc = y[pad - 3:pad - 3 + CHUNK] * cw[0:1]
            for j in range(1, CONV_W):
                acc = acc + y[pad - 3 + j:pad - 3 + j + CHUNK] * cw[j:j + 1]
            return acc * _sigmoid(acc)

        beta_all, g_all = _gdn_gates(ba_ref[rows, :], alog_ref[...], dtb_ref[...])
        gc = _cumsum_rows(tri_bf, g_all)
        gc_t = gc.T
        g_last = gc[CHUNK - 1:CHUNK, :]
        eg_all = jnp.exp(gc)
        egd_all = jnp.exp(g_last - gc)
        gl_all = jnp.exp(g_last)
        for kh in range(HK_B):
            q = _l2norm(conv_silu(kh * DK_B)) * scale
            k = _l2norm(conv_silu(QK_B + kh * DK_B))
            qb = q.astype(BF16)
            kb = k.astype(BF16)
            kk = _dot_nt(kb, kb)
            qk = _dot_nt(qb, kb)
            for j in range(rep):
                hv = kh * rep + j
                gl = HV_B + hv
                v = conv_silu(2 * QK_B + hv * DV_B)
                beta = beta_all[:, hv:hv + 1]
                g_col = gc[:, gl:gl + 1]
                g_row = gc_t[gl:gl + 1, :]
                eg = eg_all[:, gl:gl + 1]
                decay = jnp.exp(jnp.where(tril, g_col - g_row, NEG_INF))
                nmat = jnp.where(strict, -(kk * beta) * decay, 0.0)
                tinv = _unit_lower_inverse(nmat)
                rhs = jnp.concatenate([v * beta, k * (beta * eg)], axis=1)
                sol = _dot_split(tinv, rhs)
                u = sol[:, :DV_B]
                w = sol[:, DV_B:]
                attn = qk * decay
                qg = q * eg
                kdec = k * egd_all[:, gl:gl + 1]
                s = s_s[hv]
                wq = jnp.concatenate([w, qg], axis=0).astype(BF16)
                ws = _dot(wq, s.astype(BF16))
                v_new = u - ws[:CHUNK]
                vb = v_new.astype(BF16)
                o = ws[CHUNK:] + _dot(attn.astype(BF16), vb)
                s_s[hv] = gl_all[:, gl:gl + 1] * s + _dot_tn(kdec.astype(BF16), vb)
                cols = slice(hv * DV_B, (hv + 1) * DV_B)
                zz = z_ref[rows, cols].astype(F32)
                o_ref[rows, cols] = (_rms(o) * nw_ref[:, cols] * (zz * _sigmoid(zz))).astype(o_ref.dtype)
        return carry

    lax.fori_loop(0, tb // CHUNK, chunk, 0)

    xpad_s[0:pad, :] = xpad_s[tb:tb + pad, :]

    @pl.when(t == pl.num_programs(1) - 1)
    def _():
        s_out[0] = s_s[...]
        conv_out[0] = xpad_s[tb + pad - (CONV_W - 1):tb + pad, :]


def _gdn_prompt(qkv, z, ba, cw, dtb, alog, nw, batch, seq):
    tb = min(GDN_ROWS, seq)
    nt = seq // tb
    row = lambda b, t: (b * nt + t, 0)
    const = lambda b, t: (0, 0)
    return pl.pallas_call(
        _gdn_prompt_kernel,
        grid=(batch, nt),
        in_specs=[pl.BlockSpec((tb, CONV_DIM), row), pl.BlockSpec((tb, VW_B), row),
                  pl.BlockSpec((tb, LANES), row),
                  pl.BlockSpec((CONV_W, CONV_DIM), const),
                  pl.BlockSpec((1, LANES), const), pl.BlockSpec((1, LANES), const),
                  pl.BlockSpec((1, VW_B), const)],
        out_specs=[pl.BlockSpec((tb, VW_B), row),
                   pl.BlockSpec((1, HV_B, DK_B, DV_B), lambda b, t: (b, 0, 0, 0)),
                   pl.BlockSpec((1, CONV_W - 1, CONV_DIM), lambda b, t: (b, 0, 0))],
        out_shape=[jax.ShapeDtypeStruct((batch * seq, VW_B), BF16),
                   jax.ShapeDtypeStruct((batch, HV_B, DK_B, DV_B), F32),
                   jax.ShapeDtypeStruct((batch, CONV_W - 1, CONV_DIM), F32)],
        scratch_shapes=[pltpu.VMEM((tb + SUBLANES, CONV_DIM), F32),
                        pltpu.VMEM((HV_B, DK_B, DV_B), F32)],
        compiler_params=_params("parallel", "arbitrary"),
        name="gdn_prompt",
    )(qkv, z, ba, cw, dtb, alog, nw)


def _gdn_sample_kernel(xp_ref, z_ref, ba_ref, s0_ref, cw_ref, dtb_ref, alog_ref, nw_ref,
                       o_ref, s_out, kq_s, o_s, *, steps):
    xp = xp_ref[0]
    cw = cw_ref[...]
    acc = xp[0:steps] * cw[0:1]
    for j in range(1, CONV_W):
        acc = acc + xp[j:j + steps] * cw[j:j + 1]
    c = acc * _sigmoid(acc)
    scale = DK_B ** -0.5
    for kh in range(HK_B):
        cols = slice(kh * DK_B, (kh + 1) * DK_B)
        kq_s[0:steps, cols] = _l2norm(c[:, QK_B + kh * DK_B:QK_B + (kh + 1) * DK_B])
        kq_s[steps:2 * steps, cols] = _l2norm(c[:, cols]) * scale
    beta_all, g_all = _gdn_gates(ba_ref[0], alog_ref[...], dtb_ref[...])
    eg_all = jnp.exp(g_all)
    rep = HV_B // HK_B
    for kh in range(HK_B):
        kq_t = kq_s[:, kh * DK_B:(kh + 1) * DK_B].T
        for j in range(rep):
            hv = kh * rep + j
            s = s0_ref[0, hv]
            for t in range(steps):
                k_col = kq_t[:, t:t + 1]
                q_col = kq_t[:, steps + t:steps + t + 1]
                eg = eg_all[t:t + 1, HV_B + hv:HV_B + hv + 1]
                beta = beta_all[t:t + 1, hv:hv + 1]
                v_row = c[t:t + 1, 2 * QK_B + hv * DV_B:2 * QK_B + (hv + 1) * DV_B]
                sd = eg * s
                pred = jnp.sum(sd * k_col, axis=0, keepdims=True)
                s = sd + k_col * (beta * (v_row - pred))
                o_s[t:t + 1, hv * DV_B:(hv + 1) * DV_B] = jnp.sum(s * q_col, axis=0, keepdims=True)
            s_out[0, hv] = s
    for hv in range(HV_B):
        cols = slice(hv * DV_B, (hv + 1) * DV_B)
        zz = z_ref[0, :, cols]
        o_ref[0, :, cols] = _rms(o_s[0:steps, cols]) * nw_ref[:, cols] * (zz * _sigmoid(zz))


def _gdn_sample(xp, z, ba, s0, cw, dtb, alog, nw):
    batch, steps = z.shape[0], z.shape[1]
    b3 = lambda b: (b, 0, 0)
    const = lambda b: (0, 0)
    return pl.pallas_call(
        functools.partial(_gdn_sample_kernel, steps=steps),
        grid=(batch,),
        in_specs=[pl.BlockSpec((1,) + xp.shape[1:], b3),
                  pl.BlockSpec((1, steps, VW_B), b3),
                  pl.BlockSpec((1, steps, LANES), b3),
                  pl.BlockSpec((1, HV_B, DK_B, DV_B), lambda b: (b, 0, 0, 0)),
                  pl.BlockSpec((CONV_W, CONV_DIM), const),
                  pl.BlockSpec((1, LANES), const), pl.BlockSpec((1, LANES), const),
                  pl.BlockSpec((1, VW_B), const)],
        out_specs=[pl.BlockSpec((1, steps, VW_B), b3),
                   pl.BlockSpec((1, HV_B, DK_B, DV_B), lambda b: (b, 0, 0, 0))],
        out_shape=[jax.ShapeDtypeStruct((batch, steps, VW_B), F32),
                   jax.ShapeDtypeStruct((batch, HV_B, DK_B, DV_B), F32)],
        scratch_shapes=[pltpu.VMEM((2 * steps, QK_B), F32),
                        pltpu.VMEM((SUBLANES, VW_B), F32)],
        compiler_params=_params("parallel"),
        name="gdn_sample",
    )(xp, z, ba, s0, cw, dtb, alog, nw)


def _pad_cols(w, width):
    return jnp.pad(w, ((0, 0), (0, width - w.shape[1])))


def _lane_row(vec, offset):
    return jnp.zeros((1, LANES), F32).at[0, offset:offset + vec.shape[0]].set(vec.astype(F32))


MLSTM_SEGS = ((0, QK_A), (QK_A, QK_A), (2 * QK_A, VW_A), (2 * QK_A + VW_A, VW_A), (2 * QK_A + 2 * VW_A, LANES))
GDN_SEGS = ((0, CONV_DIM), (CONV_DIM, VW_B), (CONV_DIM + VW_B, LANES))


def _mlstm_layer(x, n_prompt, batch, seq, s_batch, s_seq, w_in, gate_b, norm_w, w_out, c0, n0, m0, ln_g, ln_b):
    w = _pad_cols(w_in, MLSTM_SEGS[-1][0] + LANES).astype(BF16)
    q, k, v, o, g = _proj(x, w, MLSTM_SEGS, (BF16, BF16, BF16, BF16, F32))
    gb = _lane_row(gate_b, 0)
    nw = norm_w.astype(F32).reshape(1, VW_A)
    hg_p, c_p, n_p, m_p = _mlstm_prompt(q, k, v, o, g, gb, nw, batch, seq)

    def tail(a):
        return a[n_prompt:].astype(F32).reshape(s_batch, s_seq, a.shape[1])

    kq = jnp.concatenate([tail(k), tail(q)], axis=1)
    hg_s, c_s, n_s, m_s = _mlstm_sample(kq, tail(v), tail(o), tail(g), c0, n0,
                                        m0.reshape(s_batch, 1, H_A), gb, nw)
    hg = jnp.concatenate([hg_p, hg_s.reshape(s_batch * s_seq, VW_A).astype(BF16)], axis=0)
    x = _outproj_ln(hg, x, w_out.astype(BF16), ln_g.reshape(1, -1), ln_b.reshape(1, -1))
    return x, (c_p, n_p, m_p[:, 0, :H_A]), (c_s, n_s, m_s[:, 0, :])


def _gdn_layer(x, n_prompt, batch, seq, s_batch, s_seq, w_in, conv_w, dt_bias, a_log, norm_w, w_out,
               s0, conv0, ln_g, ln_b):
    w = _pad_cols(w_in, GDN_SEGS[-1][0] + LANES).astype(BF16)
    qkv, z, ba = _proj(x, w, GDN_SEGS, (BF16, BF16, F32))
    cw = conv_w.astype(F32)
    dtb = _lane_row(dt_bias, HV_B)
    alog = _lane_row(a_log, HV_B)
    nw = norm_w.astype(F32).reshape(1, VW_B)
    o_p, s_p, conv_p = _gdn_prompt(qkv, z, ba, cw, dtb, alog, nw, batch, seq)

    def tail(a):
        return a[n_prompt:].astype(F32).reshape(s_batch, s_seq, a.shape[1])

    xp = jnp.concatenate([conv0.astype(F32), tail(qkv)], axis=1)
    conv_s = xp[:, s_seq:]
    rows = xp.shape[1]
    xp = jnp.pad(xp, ((0, 0), (0, -rows % SUBLANES), (0, 0)))
    o_s, s_s = _gdn_sample(xp, tail(z), tail(ba), s0, cw, dtb, alog, nw)
    og = jnp.concatenate([o_p, o_s.reshape(s_batch * s_seq, VW_B).astype(BF16)], axis=0)
    x = _outproj_ln(og, x, w_out.astype(BF16), ln_g.reshape(1, -1), ln_b.reshape(1, -1))
    return x, (s_p, conv_p), (s_s, conv_s)


def kernel(x_prompt, x_sample, state_mlstm_C, state_mlstm_n, state_mlstm_m, state_gdn_S, state_gdn_conv,
           a_w_in, a_gate_b, a_norm_w, a_w_out, b_w_in, b_conv_w, b_dt_bias, b_a_log, b_norm_w, b_w_out,
           mlp_w1, mlp_w2, ln1_g, ln1_b, ln2_g, ln2_b):
    batch, seq, d = x_prompt.shape
    s_batch, s_seq, _ = x_sample.shape
    n_prompt = batch * seq
    x = jnp.concatenate([x_prompt.reshape(n_prompt, d), x_sample.reshape(s_batch * s_seq, d)], axis=0)
    p_a, s_a, p_b, s_b = [], [], [], []
    for layer in range(DEPTH):
        j = layer // 2
        if layer % 2 == 0:
            x, p_state, s_state = _mlstm_layer(
                x, n_prompt, batch, seq, s_batch, s_seq, a_w_in[j], a_gate_b[j], a_norm_w[j], a_w_out[j],
                state_mlstm_C[j], state_mlstm_n[j], state_mlstm_m[j], ln1_g[layer], ln1_b[layer])
            p_a.append(p_state)
            s_a.append(s_state)
        else:
            x, p_state, s_state = _gdn_layer(
                x, n_prompt, batch, seq, s_batch, s_seq, b_w_in[j], b_conv_w[j], b_dt_bias[j], b_a_log[j],
                b_norm_w[j], b_w_out[j], state_gdn_S[j], state_gdn_conv[j], ln1_g[layer], ln1_b[layer])
            p_b.append(p_state)
            s_b.append(s_state)
        x = _mlp_ln(x, mlp_w1[layer].astype(BF16), mlp_w2[layer].astype(BF16),
                    ln2_g[layer].reshape(1, -1), ln2_b[layer].reshape(1, -1))
    y_prompt = x[:n_prompt].reshape(batch, seq, d)
    y_sample = x[n_prompt:].reshape(s_batch, s_seq, d)

    def stack(states, i):
        return jnp.stack([s[i] for s in states])

    return (y_prompt, y_sample,
            stack(p_a, 0), stack(p_a, 1), stack(p_a, 2), stack(p_b, 0), stack(p_b, 1),
            stack(s_a, 0), stack(s_a, 1), stack(s_a, 2), stack(s_b, 0), stack(s_b, 1))
```

```python
import functools

import jax
import jax.numpy as jnp
from jax import lax
from jax.experimental import pallas as pl
from jax.experimental.pallas import tpu as pltpu

F32 = jnp.float32
BF16 = jnp.bfloat16

D_MODEL = 1024
DEPTH = 4
H_A = 8
DV_A = 128
DK_A = 64
QK_A = H_A * DK_A
VW_A = H_A * DV_A
GATE_CAP = 15.0
HK_B = 8
HV_B = 16
DK_B = 128
DV_B = 128
QK_B = HK_B * DK_B
VW_B = HV_B * DV_B
CONV_W = 4
CONV_DIM = 2 * QK_B + VW_B
CHUNK = 64
D_FF = 4 * D_MODEL
ALPHA = (2.0 * DEPTH) ** 0.25
LN_EPS = 1e-5
RMS_EPS = 1e-6

LANES = 128
SUBLANES = 8
VMEM_LIMIT = 56 * 1024 * 1024
ROW_TILE = 512
FF_TILE = 1024
MLSTM_ROWS = 512
MLSTM_SEQS = 1
MLSTM_SAMPLE_SEQS = 4
SAMPLE_CHUNK = 16
GDN_ROWS = 256

NEG_INF = float("-inf")


def _params(*sem):
    return pltpu.CompilerParams(dimension_semantics=sem, vmem_limit_bytes=VMEM_LIMIT)


def _dot(a, b):
    return jnp.dot(a, b, preferred_element_type=F32)


def _dot_nt(a, b):
    return lax.dot_general(a, b, (((1,), (1,)), ((), ())), preferred_element_type=F32)


def _dot_tn(a, b):
    return lax.dot_general(a, b, (((0,), (0,)), ((), ())), preferred_element_type=F32)


def _split2(x):
    hi = x.astype(BF16)
    lo = (x - hi.astype(F32)).astype(BF16)
    return hi, lo


def _dot_split(a, b):
    ah, al = _split2(a)
    bh, bl = _split2(b)
    return _dot(ah, bh) + (_dot(ah, bl) + _dot(al, bh))


def _cumsum_rows(tri_bf, x):
    x1 = x.astype(BF16)
    r1 = x - x1.astype(F32)
    x2 = r1.astype(BF16)
    x3 = (r1 - x2.astype(F32)).astype(BF16)
    return _dot(tri_bf, x1) + (_dot(tri_bf, x2) + _dot(tri_bf, x3))


def _layer_norm(r, g, b):
    mu = jnp.mean(r, axis=-1, keepdims=True)
    d = r - mu
    var = jnp.mean(d * d, axis=-1, keepdims=True)
    return d * lax.rsqrt(var + LN_EPS) * g + b


def _rms(h):
    return h * lax.rsqrt(jnp.mean(h * h, axis=-1, keepdims=True) + RMS_EPS)


def _l2norm(x):
    return x * lax.rsqrt(jnp.sum(x * x, axis=-1, keepdims=True) + RMS_EPS)


def _sigmoid(x):
    return jax.nn.sigmoid(x)


def _proj_kernel(x_ref, w_ref, *out_refs, segs):
    xb = x_ref[...].astype(BF16)
    for (start, width), o_ref in zip(segs, out_refs):
        for c0 in range(0, width, FF_TILE):
            cw = min(FF_TILE, width - c0)
            y = _dot(xb, w_ref[:, start + c0:start + c0 + cw])
            o_ref[:, c0:c0 + cw] = y.astype(o_ref.dtype)


def _proj(x, w, segs, dtypes):
    n, k = x.shape
    grid = (n // ROW_TILE,)
    return pl.pallas_call(
        functools.partial(_proj_kernel, segs=segs),
        grid=grid,
        in_specs=[pl.BlockSpec((ROW_TILE, k), lambda i: (i, 0)),
                  pl.BlockSpec(w.shape, lambda i: (0, 0))],
        out_specs=[pl.BlockSpec((ROW_TILE, wd), lambda i: (i, 0)) for _, wd in segs],
        out_shape=[jax.ShapeDtypeStruct((n, wd), dt) for (_, wd), dt in zip(segs, dtypes)],
        compiler_params=_params("parallel"),
        name="proj",
    )(x, w)


def _outproj_ln_kernel(h_ref, x_ref, w_ref, g_ref, b_ref, o_ref):
    y = _dot(h_ref[...], w_ref[...])
    o_ref[...] = _layer_norm(ALPHA * x_ref[...] + y, g_ref[...], b_ref[...])


def _outproj_ln(h, x, w, g, b):
    n, kh = h.shape
    d = x.shape[1]
    return pl.pallas_call(
        _outproj_ln_kernel,
        grid=(n // ROW_TILE,),
        in_specs=[pl.BlockSpec((ROW_TILE, kh), lambda i: (i, 0)),
                  pl.BlockSpec((ROW_TILE, d), lambda i: (i, 0)),
                  pl.BlockSpec((kh, d), lambda i: (0, 0)),
                  pl.BlockSpec((1, d), lambda i: (0, 0)),
                  pl.BlockSpec((1, d), lambda i: (0, 0))],
        out_specs=pl.BlockSpec((ROW_TILE, d), lambda i: (i, 0)),
        out_shape=jax.ShapeDtypeStruct((n, d), F32),
        compiler_params=_params("parallel"),
        name="outproj_ln",
    )(h, x, w, g, b)


def _mlp_ln_kernel(x_ref, w1_ref, w2_ref, g_ref, b_ref, o_ref, acc_ref):
    x = x_ref[...]
    xb = x.astype(BF16)
    dff = w1_ref.shape[1]
    for c0 in range(0, dff, FF_TILE):
        h = _dot(xb, w1_ref[:, c0:c0 + FF_TILE])
        h = jnp.square(jnp.maximum(h, 0.0)).astype(BF16)
        y = _dot(h, w2_ref[c0:c0 + FF_TILE, :])
        if c0 == 0:
            acc_ref[...] = y
        else:
            acc_ref[...] += y
    o_ref[...] = _layer_norm(ALPHA * x + acc_ref[...], g_ref[...], b_ref[...])


def _mlp_ln(x, w1, w2, g, b):
    n, d = x.shape
    dff = w1.shape[1]
    return pl.pallas_call(
        _mlp_ln_kernel,
        grid=(n // ROW_TILE,),
        in_specs=[pl.BlockSpec((ROW_TILE, d), lambda i: (i, 0)),
                  pl.BlockSpec((d, dff), lambda i: (0, 0)),
                  pl.BlockSpec((dff, d), lambda i: (0, 0)),
                  pl.BlockSpec((1, d), lambda i: (0, 0)),
                  pl.BlockSpec((1, d), lambda i: (0, 0))],
        out_specs=pl.BlockSpec((ROW_TILE, d), lambda i: (i, 0)),
        out_shape=jax.ShapeDtypeStruct((n, d), F32),
        scratch_shapes=[pltpu.VMEM((ROW_TILE, d), F32)],
        compiler_params=_params("parallel"),
        name="mlp_ln",
    )(x, w1, w2, g, b)


def _tri_masks(n):
    row = lax.broadcasted_iota(jnp.int32, (n, n), 0)
    col = lax.broadcasted_iota(jnp.int32, (n, n), 1)
    return row >= col, row > col, row == col


def _mlstm_gates(g_raw, gb):
    cap = GATE_CAP * jnp.tanh((g_raw + gb) / GATE_CAP)
    return cap, jax.nn.log_sigmoid(cap)


def _mlstm_chunk(seqs, gb_ref, nw_ref, state_in, state_out, scratch, length, valid):
    c_in, n_in, m_in = state_in
    c_o, n_o, m_o = state_out
    q_s, k_s, v_s, sw_s, kwt_s, wi_s, dn_s, dc_s = scratch
    nseq = len(seqs)
    groups = nseq * H_A
    causal, _, _ = _tri_masks(length)
    tri_bf = causal.astype(BF16)
    scale = DK_A ** -0.5
    live = lax.broadcasted_iota(jnp.int32, (length, 1), 0) < valid

    gates = []
    for j, (q_ref, k_ref, v_ref, _, g_ref, rows, _) in enumerate(seqs):
        cap, lf = _mlstm_gates(g_ref[rows, :], gb_ref[...])
        if valid < length:
            cap = jnp.where(live, cap, NEG_INF)
            lf = jnp.where(live, lf, 0.0)
        bcum = _cumsum_rows(tri_bf, lf)
        gates.append((cap, cap.T, bcum, bcum.T))
        for h in range(H_A):
            q_s[j * H_A + h] = q_ref[rows, h * DK_A:(h + 1) * DK_A]
            k_s[j * H_A + h] = k_ref[rows, h * DK_A:(h + 1) * DK_A]
            v_s[j * H_A + h] = v_ref[rows, h * DV_A:(h + 1) * DV_A]
    qb = q_s[...]
    qk = _bdot_nt(qb, k_s[...])
    c_old = c_in[...].reshape(groups, DK_A, DV_A)
    qc = _bdot(qb, c_old.astype(BF16))
    head_row = lax.broadcasted_iota(jnp.int32, (H_A, DK_A), 0)
    for j in range(nseq):
        cap, cap_t, bcum, bcum_t = gates[j]
        n_all = n_in[j]
        m_all = m_in[j]
        head_lane = lax.broadcasted_iota(jnp.int32, m_all.shape, 1)
        n_next = n_all
        m_next = m_all
        for h in range(H_A):
            g = j * H_A + h
            b_col = bcum[:, H_A + h:H_A + h + 1]
            b_row = bcum_t[H_A + h:H_A + h + 1, :]
            i_col = cap[:, h:h + 1]
            i_row = cap_t[h:h + 1, :]
            m_prev = m_all[:, h:h + 1]
            dmat = jnp.where(causal, b_col - b_row + i_row, NEG_INF)
            inter = b_col + m_prev
            m_t = jnp.maximum(inter, jnp.max(dmat, axis=-1, keepdims=True))
            w_inter = jnp.exp(inter - m_t)
            s = qk[g] * scale * jnp.exp(dmat - m_t)
            n_row = n_all[h:h + 1, :]
            den = (jnp.sum(s, axis=-1, keepdims=True)
                   + w_inter * jnp.sum(q_s[g].astype(F32) * n_row, axis=-1, keepdims=True))
            sw_s[g] = s.astype(BF16)
            wi_s[g] = jnp.broadcast_to(w_inter, (length, DV_A))
            dn_s[g] = jnp.broadcast_to(jnp.maximum(jnp.abs(den), jnp.exp(-m_t)), (length, DV_A))
            m_new = m_t[length - 1:length, :]
            b_last = b_col[length - 1:length, :]
            w_k = jnp.exp(b_last - b_col + i_col - m_new)
            decay = jnp.exp(b_last + m_prev - m_new)
            kw = k_s[g].astype(F32) * scale * w_k
            kwt_s[g] = kw.T.astype(BF16)
            dc_s[g] = jnp.broadcast_to(decay, (1, DV_A))
            n_new = decay * n_row + jnp.sum(kw, axis=0, keepdims=True)
            n_next = jnp.where(head_row == h, n_new, n_next)
            m_next = jnp.where(head_lane == h, m_new, m_next)
        n_o[j] = n_next
        m_o[j] = m_next
    vb = v_s[...]
    num = _bdot(sw_s[...], vb) + wi_s[...] * qc
    c_o[...] = (dc_s[...] * c_old + _bdot(kwt_s[...], vb)).reshape(c_o.shape)
    hn = _rms(num / dn_s[...]) * nw_ref[...]
    for j, (_, _, _, o_ref, _, rows, write_hg) in enumerate(seqs):
        for h in range(H_A):
            cols = slice(h * DV_A, (h + 1) * DV_A)
            gate = _sigmoid(o_ref[rows, cols].astype(F32))
            write_hg(cols, gate * hn[j * H_A + h])


def _mlstm_scratch(groups, length):
    return [pltpu.VMEM((groups, length, DK_A), BF16),
            pltpu.VMEM((groups, length, DK_A), BF16),
            pltpu.VMEM((groups, length, DV_A), BF16),
            pltpu.VMEM((groups, length, length), BF16),
            pltpu.VMEM((groups, DK_A, length), BF16),
            pltpu.VMEM((groups, length, DV_A), F32),
            pltpu.VMEM((groups, length, DV_A), F32),
            pltpu.VMEM((groups, 1, DV_A), F32)]


def _mlstm_prompt_kernel(*refs, nseq):
    ins = refs[:5 * nseq]
    gb_ref, nw_ref, hg_ref, c_out, n_out, m_out, c_s, n_s, m_s = refs[5 * nseq:5 * nseq + 9]
    scratch = refs[5 * nseq + 9:]
    t = pl.program_id(1)

    @pl.when(t == 0)
    def _():
        c_s[...] = jnp.zeros_like(c_s)
        n_s[...] = jnp.zeros_like(n_s)
        m_s[...] = jnp.zeros_like(m_s)

    def chunk(c, carry):
        rows = pl.ds(pl.multiple_of(c * CHUNK, CHUNK), CHUNK)
        seqs = []
        for j in range(nseq):
            def write_hg(cols, val, j=j):
                hg_ref[j, rows, cols] = val.astype(hg_ref.dtype)
            seqs.append(tuple(ins[i * nseq + j] for i in range(5)) + (rows, write_hg))
        state = (c_s, n_s, m_s)
        _mlstm_chunk(seqs, gb_ref, nw_ref, state, state, scratch, CHUNK, CHUNK)
        return carry

    lax.fori_loop(0, hg_ref.shape[1] // CHUNK, chunk, 0)

    @pl.when(t == pl.num_programs(1) - 1)
    def _():
        c_out[...] = c_s[...].reshape(c_out.shape)
        n_out[...] = n_s[...]
        m_out[...] = m_s[...]


def _mlstm_prompt(q, k, v, o, g, gb, nw, batch, seq):
    tb = min(MLSTM_ROWS, seq)
    nt = seq // tb
    nseq = MLSTM_SEQS if batch % MLSTM_SEQS == 0 else 1
    groups = nseq * H_A
    const = lambda b, t: (0, 0)

    def rows_of(j):
        return lambda b, t: ((b * nseq + j) * nt + t, 0)

    in_specs, args = [], []
    for arr, width in ((q, QK_A), (k, QK_A), (v, VW_A), (o, VW_A), (g, LANES)):
        for j in range(nseq):
            in_specs.append(pl.BlockSpec((tb, width), rows_of(j)))
            args.append(arr)
    in_specs += [pl.BlockSpec((1, LANES), const), pl.BlockSpec((groups, 1, DV_A), lambda b, t: (0, 0, 0))]
    args += [gb, jnp.tile(nw.reshape(H_A, 1, DV_A), (nseq, 1, 1))]
    hg, c, n, m = pl.pallas_call(
        functools.partial(_mlstm_prompt_kernel, nseq=nseq),
        grid=(batch // nseq, nt),
        in_specs=in_specs,
        out_specs=[pl.BlockSpec((nseq, tb, VW_A), lambda b, t: (b, t, 0)),
                   pl.BlockSpec((nseq, H_A, DK_A, DV_A), lambda b, t: (b, 0, 0, 0)),
                   pl.BlockSpec((nseq, H_A, DK_A), lambda b, t: (b, 0, 0)),
                   pl.BlockSpec((nseq, 1, LANES), lambda b, t: (b, 0, 0))],
        out_shape=[jax.ShapeDtypeStruct((batch, seq, VW_A), BF16),
                   jax.ShapeDtypeStruct((batch, H_A, DK_A, DV_A), F32),
                   jax.ShapeDtypeStruct((batch, H_A, DK_A), F32),
                   jax.ShapeDtypeStruct((batch, 1, LANES), F32)],
        scratch_shapes=[pltpu.VMEM((groups, DK_A, DV_A), F32),
                        pltpu.VMEM((nseq, H_A, DK_A), F32),
                        pltpu.VMEM((nseq, 1, LANES), F32)]
        + _mlstm_scratch(groups, CHUNK),
        compiler_params=_params("parallel", "arbitrary"),
        name="mlstm_prompt",
    )(*args)
    return hg.reshape(batch * seq, VW_A), c, n, m


def _mlstm_sample_kernel(q_ref, k_ref, v_ref, o_ref, g_ref, c0_ref, n0_ref, m0_ref, gb_ref, nw_ref,
                         hg_ref, c_out, n_out, m_out, *scratch, nseq, length, valid):
    seqs = []
    for j in range(nseq):
        rows = slice(j * length, (j + 1) * length)

        def write_hg(cols, val, rows=rows):
            hg_ref[rows, cols] = val.astype(hg_ref.dtype)
        seqs.append((q_ref, k_ref, v_ref, o_ref, g_ref, rows, write_hg))
    _mlstm_chunk(seqs, gb_ref, nw_ref, (c0_ref, n0_ref, m0_ref), (c_out, n_out, m_out), scratch, length, valid)


def _mlstm_sample(q, k, v, o, g, c0, n0, m0, gb, nw, length, valid):
    batch = c0.shape[0]
    nseq = MLSTM_SAMPLE_SEQS if batch % MLSTM_SAMPLE_SEQS == 0 else 1
    groups = nseq * H_A
    rows = nseq * length
    row = lambda b: (b, 0)
    b3 = lambda b: (b, 0, 0)
    return pl.pallas_call(
        functools.partial(_mlstm_sample_kernel, nseq=nseq, length=length, valid=valid),
        grid=(batch // nseq,),
        in_specs=[pl.BlockSpec((rows, QK_A), row), pl.BlockSpec((rows, QK_A), row),
                  pl.BlockSpec((rows, VW_A), row), pl.BlockSpec((rows, VW_A), row),
                  pl.BlockSpec((rows, LANES), row),
                  pl.BlockSpec((nseq, H_A, DK_A, DV_A), lambda b: (b, 0, 0, 0)),
                  pl.BlockSpec((nseq, H_A, DK_A), b3),
                  pl.BlockSpec((nseq, 1, H_A), b3),
                  pl.BlockSpec((1, LANES), lambda b: (0, 0)),
                  pl.BlockSpec((groups, 1, DV_A), lambda b: (0, 0, 0))],
        out_specs=[pl.BlockSpec((rows, VW_A), row),
                   pl.BlockSpec((nseq, H_A, DK_A, DV_A), lambda b: (b, 0, 0, 0)),
                   pl.BlockSpec((nseq, H_A, DK_A), b3),
                   pl.BlockSpec((nseq, 1, H_A), b3)],
        out_shape=[jax.ShapeDtypeStruct((batch * length, VW_A), BF16),
                   jax.ShapeDtypeStruct((batch, H_A, DK_A, DV_A), F32),
                   jax.ShapeDtypeStruct((batch, H_A, DK_A), F32),
                   jax.ShapeDtypeStruct((batch, 1, H_A), F32)],
        scratch_shapes=_mlstm_scratch(groups, length),
        compiler_params=_params("parallel"),
        name="mlstm_sample",
    )(q, k, v, o, g, c0, n0, m0, gb, jnp.tile(nw.reshape(H_A, 1, DV_A), (nseq, 1, 1)))


def _gdn_gates(ba, alog, dtb):
    beta = _sigmoid(ba)
    g = -jnp.exp(alog) * jax.nn.softplus(ba + dtb)
    return beta, g


def _bdot(a, b):
    return jnp.einsum('hmk,hkn->hmn', a, b, preferred_element_type=F32)


def _bdot_nt(a, b):
    return jnp.einsum('hmk,hnk->hmn', a, b, preferred_element_type=F32)


def _unit_lower_solve(nmat, x):
    size = nmat.shape[-1]
    span = 1
    while span < size:
        nh, nl = _split2(nmat)
        xh, xl = _split2(x)
        x = x + (_bdot(nh, xh) + (_bdot(nh, xl) + _bdot(nl, xh)))
        span *= 2
        if span < size:
            nmat = _bdot(nh, nh) + (_bdot(nh, nl) + _bdot(nl, nh))
    return x


def _gdn_prompt_kernel(qkv_ref, z_ref, ba_ref, cw_ref, dtb_ref, alog_ref, nw_ref,
                       o_ref, s_out, conv_out,
                       xpad_s, s_s, q_s, k_s, x_s, n_s, attn_s, wq_s, kdt_s, gl_s):
    t = pl.program_id(1)
    tb = qkv_ref.shape[0]
    pad = SUBLANES

    @pl.when(t == 0)
    def _():
        s_s[...] = jnp.zeros_like(s_s)
        xpad_s[0:pad, :] = jnp.zeros((pad, CONV_DIM), F32)

    xpad_s[pad:pad + tb, :] = qkv_ref[...].astype(F32)

    tril, strict, _ = _tri_masks(CHUNK)
    tri_bf = tril.astype(BF16)
    scale = DK_B ** -0.5
    rep = HV_B // HK_B

    def chunk(c, carry):
        r0 = pl.multiple_of(c * CHUNK, CHUNK)
        rows = pl.ds(r0, CHUNK)

        def conv_silu(col0):
            y = xpad_s[pl.ds(r0, CHUNK + pad), col0:col0 + LANES]
            cw = cw_ref[:, col0:col0 + LANES]
            acc = y[pad - 3:pad - 3 + CHUNK] * cw[0:1]
            for j in range(1, CONV_W):
                acc = acc + y[pad - 3 + j:pad - 3 + j + CHUNK] * cw[j:j + 1]
            return acc * _sigmoid(acc)

        beta_all, g_all = _gdn_gates(ba_ref[rows, :], alog_ref[...], dtb_ref[...])
        gc = _cumsum_rows(tri_bf, g_all)
        gc_t = gc.T
        g_last = gc[CHUNK - 1:CHUNK, :]
        eg_all = jnp.exp(gc)
        egd_all = jnp.exp(g_last - gc)
        gl_all = jnp.exp(g_last)

        for kh in range(HK_B):
            q_s[kh] = _l2norm(conv_silu(kh * DK_B)) * scale
            k_s[kh] = _l2norm(conv_silu(QK_B + kh * DK_B))
        kb = k_s[...].astype(BF16)
        kk = _bdot_nt(kb, kb)
        qk = _bdot_nt(q_s[...].astype(BF16), kb)

        for hv in range(HV_B):
            kh = hv // rep
            gl = HV_B + hv
            v = conv_silu(2 * QK_B + hv * DV_B)
            k = k_s[kh]
            beta = beta_all[:, hv:hv + 1]
            eg = eg_all[:, gl:gl + 1]
            decay = jnp.exp(jnp.where(tril, gc[:, gl:gl + 1] - gc_t[gl:gl + 1, :], NEG_INF))
            n_s[hv] = jnp.where(strict, -(kk[kh] * beta) * decay, 0.0)
            x_s[hv, :, 0:DV_B] = v * beta
            x_s[hv, :, DV_B:2 * DV_B] = k * (beta * eg)
            attn_s[hv] = (qk[kh] * decay).astype(BF16)
            wq_s[hv, CHUNK:2 * CHUNK, :] = (q_s[kh] * eg).astype(BF16)
            kdt_s[hv] = (k * egd_all[:, gl:gl + 1]).T.astype(BF16)
            gl_s[hv] = jnp.broadcast_to(gl_all[:, gl:gl + 1], (1, LANES))

        sol = _unit_lower_solve(n_s[...], x_s[...])
        u = sol[:, :, 0:DV_B]
        wq_s[:, 0:CHUNK, :] = sol[:, :, DV_B:2 * DV_B].astype(BF16)

        s = s_s[...]
        ws = _bdot(wq_s[...], s.astype(BF16))
        vb = (u - ws[:, 0:CHUNK, :]).astype(BF16)
        o = ws[:, CHUNK:2 * CHUNK, :] + _bdot(attn_s[...], vb)
        s_s[...] = gl_s[...] * s + _bdot(kdt_s[...], vb)
        for hv in range(HV_B):
            cols = slice(hv * DV_B, (hv + 1) * DV_B)
            zz = z_ref[rows, cols].astype(F32)
            o_ref[rows, cols] = (_rms(o[hv]) * nw_ref[:, cols] * (zz * _sigmoid(zz))).astype(o_ref.dtype)
        return carry

    lax.fori_loop(0, tb // CHUNK, chunk, 0)

    xpad_s[0:pad, :] = xpad_s[tb:tb + pad, :]

    @pl.when(t == pl.num_programs(1) - 1)
    def _():
        s_out[0] = s_s[...]
        conv_out[0] = xpad_s[tb + pad - (CONV_W - 1):tb + pad, :]


def _gdn_prompt(qkv, z, ba, cw, dtb, alog, nw, batch, seq):
    tb = min(GDN_ROWS, seq)
    nt = seq // tb
    row = lambda b, t: (b * nt + t, 0)
    const = lambda b, t: (0, 0)
    return pl.pallas_call(
        _gdn_prompt_kernel,
        grid=(batch, nt),
        in_specs=[pl.BlockSpec((tb, CONV_DIM), row), pl.BlockSpec((tb, VW_B), row),
                  pl.BlockSpec((tb, LANES), row),
                  pl.BlockSpec((CONV_W, CONV_DIM), const),
                  pl.BlockSpec((1, LANES), const), pl.BlockSpec((1, LANES), const),
                  pl.BlockSpec((1, VW_B), const)],
        out_specs=[pl.BlockSpec((tb, VW_B), row),
                   pl.BlockSpec((1, HV_B, DK_B, DV_B), lambda b, t: (b, 0, 0, 0)),
                   pl.BlockSpec((1, CONV_W - 1, CONV_DIM), lambda b, t: (b, 0, 0))],
        out_shape=[jax.ShapeDtypeStruct((batch * seq, VW_B), BF16),
                   jax.ShapeDtypeStruct((batch, HV_B, DK_B, DV_B), F32),
                   jax.ShapeDtypeStruct((batch, CONV_W - 1, CONV_DIM), F32)],
        scratch_shapes=[pltpu.VMEM((tb + SUBLANES, CONV_DIM), F32),
                        pltpu.VMEM((HV_B, DK_B, DV_B), F32),
                        pltpu.VMEM((HK_B, CHUNK, DK_B), F32),
                        pltpu.VMEM((HK_B, CHUNK, DK_B), F32),
                        pltpu.VMEM((HV_B, CHUNK, 2 * DV_B), F32),
                        pltpu.VMEM((HV_B, CHUNK, CHUNK), F32),
                        pltpu.VMEM((HV_B, CHUNK, CHUNK), BF16),
                        pltpu.VMEM((HV_B, 2 * CHUNK, DK_B), BF16),
                        pltpu.VMEM((HV_B, DK_B, CHUNK), BF16),
                        pltpu.VMEM((HV_B, 1, LANES), F32)],
        compiler_params=_params("parallel", "arbitrary"),
        name="gdn_prompt",
    )(qkv, z, ba, cw, dtb, alog, nw)


def _gdn_sample_kernel(xp_ref, z_ref, ba_ref, s0_ref, cw_ref, dtb_ref, alog_ref, nw_ref,
                       o_ref, s_out, kq_s, o_s, *, steps):
    xp = xp_ref[0]
    cw = cw_ref[...]
    acc = xp[0:steps] * cw[0:1]
    for j in range(1, CONV_W):
        acc = acc + xp[j:j + steps] * cw[j:j + 1]
    c = acc * _sigmoid(acc)
    scale = DK_B ** -0.5
    for kh in range(HK_B):
        cols = slice(kh * DK_B, (kh + 1) * DK_B)
        kq_s[0:steps, cols] = _l2norm(c[:, QK_B + kh * DK_B:QK_B + (kh + 1) * DK_B])
        kq_s[steps:2 * steps, cols] = _l2norm(c[:, cols]) * scale
    beta_all, g_all = _gdn_gates(ba_ref[0], alog_ref[...], dtb_ref[...])
    eg_all = jnp.exp(g_all)
    rep = HV_B // HK_B
    for kh in range(HK_B):
        kq_t = kq_s[:, kh * DK_B:(kh + 1) * DK_B].T
        for j in range(rep):
            hv = kh * rep + j
            s = s0_ref[0, hv]
            for t in range(steps):
                k_col = kq_t[:, t:t + 1]
                q_col = kq_t[:, steps + t:steps + t + 1]
                eg = eg_all[t:t + 1, HV_B + hv:HV_B + hv + 1]
                beta = beta_all[t:t + 1, hv:hv + 1]
                v_row = c[t:t + 1, 2 * QK_B + hv * DV_B:2 * QK_B + (hv + 1) * DV_B]
                sd = eg * s
                pred = jnp.sum(sd * k_col, axis=0, keepdims=True)
                s = sd + k_col * (beta * (v_row - pred))
                o_s[t:t + 1, hv * DV_B:(hv + 1) * DV_B] = jnp.sum(s * q_col, axis=0, keepdims=True)
            s_out[0, hv] = s
    for hv in range(HV_B):
        cols = slice(hv * DV_B, (hv + 1) * DV_B)
        zz = z_ref[0, :, cols]
        o_ref[0, :, cols] = _rms(o_s[0:steps, cols]) * nw_ref[:, cols] * (zz * _sigmoid(zz))


def _gdn_sample(xp, z, ba, s0, cw, dtb, alog, nw):
    batch, steps = z.shape[0], z.shape[1]
    b3 = lambda b: (b, 0, 0)
    const = lambda b: (0, 0)
    return pl.pallas_call(
        functools.partial(_gdn_sample_kernel, steps=steps),
        grid=(batch,),
        in_specs=[pl.BlockSpec((1,) + xp.shape[1:], b3),
                  pl.BlockSpec((1, steps, VW_B), b3),
                  pl.BlockSpec((1, steps, LANES), b3),
                  pl.BlockSpec((1, HV_B, DK_B, DV_B), lambda b: (b, 0, 0, 0)),
                  pl.BlockSpec((CONV_W, CONV_DIM), const),
                  pl.BlockSpec((1, LANES), const), pl.BlockSpec((1, LANES), const),
                  pl.BlockSpec((1, VW_B), const)],
        out_specs=[pl.BlockSpec((1, steps, VW_B), b3),
                   pl.BlockSpec((1, HV_B, DK_B, DV_B), lambda b: (b, 0, 0, 0))],
        out_shape=[jax.ShapeDtypeStruct((batch, steps, VW_B), F32),
                   jax.ShapeDtypeStruct((batch, HV_B, DK_B, DV_B), F32)],
        scratch_shapes=[pltpu.VMEM((2 * steps, QK_B), F32),
                        pltpu.VMEM((SUBLANES, VW_B), F32)],
        compiler_params=_params("parallel"),
        name="gdn_sample",
    )(xp, z, ba, s0, cw, dtb, alog, nw)


def _pad_cols(w, width):
    return jnp.pad(w, ((0, 0), (0, width - w.shape[1])))


def _lane_row(vec, offset):
    return jnp.zeros((1, LANES), F32).at[0, offset:offset + vec.shape[0]].set(vec.astype(F32))


MLSTM_SEGS = ((0, QK_A), (QK_A, QK_A), (2 * QK_A, VW_A), (2 * QK_A + VW_A, VW_A), (2 * QK_A + 2 * VW_A, LANES))
GDN_SEGS = ((0, CONV_DIM), (CONV_DIM, VW_B), (CONV_DIM + VW_B, LANES))


def _mlstm_layer(x, n_prompt, batch, seq, s_batch, s_seq, w_in, gate_b, norm_w, w_out, c0, n0, m0, ln_g, ln_b):
    w = _pad_cols(w_in, MLSTM_SEGS[-1][0] + LANES).astype(BF16)
    q, k, v, o, g = _proj(x, w, MLSTM_SEGS, (BF16, BF16, BF16, BF16, F32))
    gb = _lane_row(gate_b, 0)
    nw = norm_w.astype(F32).reshape(1, VW_A)
    hg_p, c_p, n_p, m_p = _mlstm_prompt(q, k, v, o, g, gb, nw, batch, seq)

    def tail(a):
        a = a[n_prompt:].reshape(s_batch, s_seq, a.shape[1])
        return jnp.pad(a, ((0, 0), (0, SAMPLE_CHUNK - s_seq), (0, 0))).reshape(s_batch * SAMPLE_CHUNK, a.shape[2])

    hg_s, c_s, n_s, m_s = _mlstm_sample(tail(q), tail(k), tail(v), tail(o), tail(g), c0, n0,
                                        m0.reshape(s_batch, 1, H_A), gb, nw, SAMPLE_CHUNK, s_seq)
    hg_s = hg_s.reshape(s_batch, SAMPLE_CHUNK, VW_A)[:, :s_seq].reshape(s_batch * s_seq, VW_A)
    hg = jnp.concatenate([hg_p, hg_s], axis=0)
    x = _outproj_ln(hg, x, w_out.astype(BF16), ln_g.reshape(1, -1), ln_b.reshape(1, -1))
    return x, (c_p, n_p, m_p[:, 0, :H_A]), (c_s, n_s, m_s[:, 0, :])


def _gdn_layer(x, n_prompt, batch, seq, s_batch, s_seq, w_in, conv_w, dt_bias, a_log, norm_w, w_out,
               s0, conv0, ln_g, ln_b):
    w = _pad_cols(w_in, GDN_SEGS[-1][0] + LANES).astype(BF16)
    qkv, z, ba = _proj(x, w, GDN_SEGS, (BF16, BF16, F32))
    cw = conv_w.astype(F32)
    dtb = _lane_row(dt_bias, HV_B)
    alog = _lane_row(a_log, HV_B)
    nw = norm_w.astype(F32).reshape(1, VW_B)
    o_p, s_p, conv_p = _gdn_prompt(qkv, z, ba, cw, dtb, alog, nw, batch, seq)

    def tail(a):
        return a[n_prompt:].astype(F32).reshape(s_batch, s_seq, a.shape[1])

    xp = jnp.concatenate([conv0.astype(F32), tail(qkv)], axis=1)
    conv_s = xp[:, s_seq:]
    rows = xp.shape[1]
    xp = jnp.pad(xp, ((0, 0), (0, -rows % SUBLANES), (0, 0)))
    o_s, s_s = _gdn_sample(xp, tail(z), tail(ba), s0, cw, dtb, alog, nw)
    og = jnp.concatenate([o_p, o_s.reshape(s_batch * s_seq, VW_B).astype(BF16)], axis=0)
    x = _outproj_ln(og, x, w_out.astype(BF16), ln_g.reshape(1, -1), ln_b.reshape(1, -1))
    return x, (s_p, conv_p), (s_s, conv_s)


def kernel(x_prompt, x_sample, state_mlstm_C, state_mlstm_n, state_mlstm_m, state_gdn_S, state_gdn_conv,
           a_w_in, a_gate_b, a_norm_w, a_w_out, b_w_in, b_conv_w, b_dt_bias, b_a_log, b_norm_w, b_w_out,
           mlp_w1, mlp_w2, ln1_g, ln1_b, ln2_g, ln2_b):
    batch, seq, d = x_prompt.shape
    s_batch, s_seq, _ = x_sample.shape
    n_prompt = batch * seq
    x = jnp.concatenate([x_prompt.reshape(n_prompt, d), x_sample.reshape(s_batch * s_seq, d)], axis=0)
    p_a, s_a, p_b, s_b = [], [], [], []
    for layer in range(DEPTH):
        j = layer // 2
        if layer % 2 == 0:
            x, p_state, s_state = _mlstm_layer(
                x, n_prompt, batch, seq, s_batch, s_seq, a_w_in[j], a_gate_b[j], a_norm_w[j], a_w_out[j],
                state_mlstm_C[j], state_mlstm_n[j], state_mlstm_m[j], ln1_g[layer], ln1_b[layer])
            p_a.append(p_state)
            s_a.append(s_state)
        else:
            x, p_state, s_state = _gdn_layer(
                x, n_prompt, batch, seq, s_batch, s_seq, b_w_in[j], b_conv_w[j], b_dt_bias[j], b_a_log[j],
                b_norm_w[j], b_w_out[j], state_gdn_S[j], state_gdn_conv[j], ln1_g[layer], ln1_b[layer])
            p_b.append(p_state)
            s_b.append(s_state)
        x = _mlp_ln(x, mlp_w1[layer].astype(BF16), mlp_w2[layer].astype(BF16),
                    ln2_g[layer].reshape(1, -1), ln2_b[layer].reshape(1, -1))
    y_prompt = x[:n_prompt].reshape(batch, seq, d)
    y_sample = x[n_prompt:].reshape(s_batch, s_seq, d)

    def stack(states, i):
        return jnp.stack([s[i] for s in states])

    return (y_prompt, y_sample,
            stack(p_a, 0), stack(p_a, 1), stack(p_a, 2), stack(p_b, 0), stack(p_b, 1),
            stack(s_a, 0), stack(s_a, 1), stack(s_a, 2), stack(s_b, 0), stack(s_b, 1))
```

```python
import functools

import jax
import jax.numpy as jnp
from jax import lax
from jax.experimental import pallas as pl
from jax.experimental.pallas import tpu as pltpu

F32 = jnp.float32
BF16 = jnp.bfloat16

D_MODEL = 1024
DEPTH = 4
H_A = 8
DV_A = 128
DK_A = 64
QK_A = H_A * DK_A
VW_A = H_A * DV_A
GATE_CAP = 15.0
HK_B = 8
HV_B = 16
DK_B = 128
DV_B = 128
QK_B = HK_B * DK_B
VW_B = HV_B * DV_B
CONV_W = 4
CONV_DIM = 2 * QK_B + VW_B
CHUNK = 64
D_FF = 4 * D_MODEL
ALPHA = (2.0 * DEPTH) ** 0.25
LN_EPS = 1e-5
RMS_EPS = 1e-6

LANES = 128
SUBLANES = 8
VMEM_LIMIT = 56 * 1024 * 1024
ROW_TILE = 512
FF_TILE = 1024
MLSTM_ROWS = 512
MLSTM_SEQS = 1
MLSTM_SAMPLE_SEQS = 4
SAMPLE_CHUNK = 16
GDN_ROWS = 256

NEG_INF = float("-inf")


def _params(*sem):
    return pltpu.CompilerParams(dimension_semantics=sem, vmem_limit_bytes=VMEM_LIMIT)


def _dot(a, b):
    return jnp.dot(a, b, preferred_element_type=F32)


def _dot_nt(a, b):
    return lax.dot_general(a, b, (((1,), (1,)), ((), ())), preferred_element_type=F32)


def _dot_tn(a, b):
    return lax.dot_general(a, b, (((0,), (0,)), ((), ())), preferred_element_type=F32)


def _split2(x):
    hi = x.astype(BF16)
    lo = (x - hi.astype(F32)).astype(BF16)
    return hi, lo


def _dot_split(a, b):
    ah, al = _split2(a)
    bh, bl = _split2(b)
    return _dot(ah, bh) + (_dot(ah, bl) + _dot(al, bh))


def _cumsum_rows(tri_bf, x):
    x1 = x.astype(BF16)
    r1 = x - x1.astype(F32)
    x2 = r1.astype(BF16)
    x3 = (r1 - x2.astype(F32)).astype(BF16)
    return _dot(tri_bf, x1) + (_dot(tri_bf, x2) + _dot(tri_bf, x3))


def _layer_norm(r, g, b):
    mu = jnp.mean(r, axis=-1, keepdims=True)
    d = r - mu
    var = jnp.mean(d * d, axis=-1, keepdims=True)
    return d * lax.rsqrt(var + LN_EPS) * g + b


def _rms(h):
    return h * lax.rsqrt(jnp.mean(h * h, axis=-1, keepdims=True) + RMS_EPS)


def _l2norm(x):
    return x * lax.rsqrt(jnp.sum(x * x, axis=-1, keepdims=True) + RMS_EPS)


def _sigmoid(x):
    return jax.nn.sigmoid(x)


def _pair_specs(pair):
    main, tail = pair
    last = main.shape[0] // ROW_TILE - 1
    width = main.shape[1]
    return [pl.BlockSpec((ROW_TILE, width), lambda i: (jnp.minimum(i, last), 0)),
            pl.BlockSpec((ROW_TILE, width), lambda i: (0, 0))]


def _pair_shapes(pair_like, width, dtype):
    main, tail = pair_like
    return [jax.ShapeDtypeStruct((main.shape[0], width), dtype), jax.ShapeDtypeStruct((tail.shape[0], width), dtype)]


def _pair_load(main_ref, tail_ref):
    on_tail = pl.program_id(0) == pl.num_programs(0) - 1
    return jnp.where(on_tail, tail_ref[...], main_ref[...])


def _pair_store(main_ref, tail_ref, cols, val):
    on_tail = pl.program_id(0) == pl.num_programs(0) - 1

    @pl.when(jnp.logical_not(on_tail))
    def _():
        main_ref[:, cols] = val.astype(main_ref.dtype)

    @pl.when(on_tail)
    def _():
        tail_ref[:, cols] = val.astype(tail_ref.dtype)


def _proj_kernel(xm_ref, xt_ref, w_ref, *out_refs, segs):
    xb = _pair_load(xm_ref, xt_ref).astype(BF16)
    for s, (start, width) in enumerate(segs):
        for c0 in range(0, width, FF_TILE):
            cw = min(FF_TILE, width - c0)
            y = _dot(xb, w_ref[:, start + c0:start + c0 + cw])
            _pair_store(out_refs[2 * s], out_refs[2 * s + 1], slice(c0, c0 + cw), y)


def _proj(x, w, segs, dtypes):
    specs, shapes = [], []
    for (_, wd), dt in zip(segs, dtypes):
        shapes += _pair_shapes(x, wd, dt)
        specs += _pair_specs(shapes[-2:])
    outs = pl.pallas_call(
        functools.partial(_proj_kernel, segs=segs),
        grid=(x[0].shape[0] // ROW_TILE + 1,),
        in_specs=_pair_specs(x) + [pl.BlockSpec(w.shape, lambda i: (0, 0))],
        out_specs=specs,
        out_shape=shapes,
        compiler_params=_params("arbitrary"),
        name="proj",
    )(*x, w)
    return [(outs[2 * s], outs[2 * s + 1]) for s in range(len(segs))]


def _outproj_ln_kernel(hm_ref, ht_ref, xm_ref, xt_ref, w_ref, g_ref, b_ref, om_ref, ot_ref):
    y = _dot(_pair_load(hm_ref, ht_ref), w_ref[...])
    res = _layer_norm(ALPHA * _pair_load(xm_ref, xt_ref) + y, g_ref[...], b_ref[...])
    _pair_store(om_ref, ot_ref, slice(None), res)


def _outproj_ln(h, x, w, g, b):
    kh, d = w.shape
    const = lambda i: (0, 0)
    shapes = _pair_shapes(x, d, F32)
    return tuple(pl.pallas_call(
        _outproj_ln_kernel,
        grid=(x[0].shape[0] // ROW_TILE + 1,),
        in_specs=_pair_specs(h) + _pair_specs(x) + [pl.BlockSpec((kh, d), const),
                                                    pl.BlockSpec((1, d), const), pl.BlockSpec((1, d), const)],
        out_specs=_pair_specs(shapes),
        out_shape=shapes,
        compiler_params=_params("arbitrary"),
        name="outproj_ln",
    )(*h, *x, w, g, b))


def _mlp_ln_kernel(xm_ref, xt_ref, w1_ref, w2_ref, g_ref, b_ref, om_ref, ot_ref, acc_ref):
    x = _pair_load(xm_ref, xt_ref)
    xb = x.astype(BF16)
    dff = w1_ref.shape[1]
    for c0 in range(0, dff, FF_TILE):
        h = _dot(xb, w1_ref[:, c0:c0 + FF_TILE])
        h = jnp.square(jnp.maximum(h, 0.0)).astype(BF16)
        y = _dot(h, w2_ref[c0:c0 + FF_TILE, :])
        if c0 == 0:
            acc_ref[...] = y
        else:
            acc_ref[...] += y
    res = _layer_norm(ALPHA * x + acc_ref[...], g_ref[...], b_ref[...])
    _pair_store(om_ref, ot_ref, slice(None), res)


def _mlp_ln(x, w1, w2, g, b):
    d, dff = w1.shape
    const = lambda i: (0, 0)
    shapes = _pair_shapes(x, d, F32)
    return tuple(pl.pallas_call(
        _mlp_ln_kernel,
        grid=(x[0].shape[0] // ROW_TILE + 1,),
        in_specs=_pair_specs(x) + [pl.BlockSpec((d, dff), const), pl.BlockSpec((dff, d), const),
                                   pl.BlockSpec((1, d), const), pl.BlockSpec((1, d), const)],
        out_specs=_pair_specs(shapes),
        out_shape=shapes,
        scratch_shapes=[pltpu.VMEM((ROW_TILE, d), F32)],
        compiler_params=_params("arbitrary"),
        name="mlp_ln",
    )(*x, w1, w2, g, b))


def _tri_masks(n):
    row = lax.broadcasted_iota(jnp.int32, (n, n), 0)
    col = lax.broadcasted_iota(jnp.int32, (n, n), 1)
    return row >= col, row > col, row == col


def _mlstm_gates(g_raw, gb):
    cap = GATE_CAP * jnp.tanh((g_raw + gb) / GATE_CAP)
    return cap, jax.nn.log_sigmoid(cap)


def _mlstm_chunk(seqs, gb_ref, nw_ref, state_in, state_out, scratch, length, valid):
    c_in, nb_in, m_in = state_in
    c_o, nb_o, m_o = state_out
    q_s, k_s, vx_s, e_s, kwt_s, wi_s, em_s, dc_s = scratch
    nseq = len(seqs)
    groups = nseq * H_A
    causal, _, _ = _tri_masks(length)
    tri_bf = causal.astype(BF16)
    scale = DK_A ** -0.5
    row_id = lax.broadcasted_iota(jnp.int32, (length, LANES), 0)
    lane_id = lax.broadcasted_iota(jnp.int32, (length, LANES), 1)
    head_lanes = (lane_id >= H_A) & (lane_id < 2 * H_A)
    ones = jnp.ones((length, DV_A), BF16)

    for j, (q_ref, k_ref, v_ref, _, g_ref, rows, _) in enumerate(seqs):
        cap, lf = _mlstm_gates(g_ref[rows, :], gb_ref[...])
        i_sh = pltpu.roll(cap, H_A, axis=1)
        if valid < length:
            i_sh = jnp.where(row_id < valid, i_sh, NEG_INF)
            lf = jnp.where(row_id < valid, lf, 0.0)
        i_sh = jnp.where(head_lanes, i_sh, 0.0)
        bcum = jnp.where(head_lanes, _cumsum_rows(tri_bf, lf), 0.0)
        cmat = i_sh - bcum
        cmax = cmat
        span = 1
        while span < length:
            cmax = jnp.maximum(cmax, jnp.where(row_id >= span, pltpu.roll(cmax, span, axis=0), NEG_INF))
            span *= 2
        m_prev = m_in[j]
        m_t = bcum + jnp.maximum(m_prev, cmax)
        a_mat = bcum - m_t
        w_inter = jnp.exp(bcum + m_prev - m_t)
        e_m = jnp.exp(-m_t)
        m_new = m_t[length - 1:length, :]
        b_last = bcum[length - 1:length, :]
        w_k = jnp.exp(b_last - bcum + i_sh - m_new)
        decay = jnp.exp(b_last + m_prev - m_new)
        m_o[j] = m_new
        cmat_t = cmat.T
        for h in range(H_A):
            g = j * H_A + h
            lane = H_A + h
            q_s[g] = q_ref[rows, h * DK_A:(h + 1) * DK_A]
            kh = k_ref[rows, h * DK_A:(h + 1) * DK_A]
            k_s[g] = kh
            vx_s[g, :, 0:DV_A] = v_ref[rows, h * DV_A:(h + 1) * DV_A]
            vx_s[g, :, DV_A:2 * DV_A] = ones
            e_s[g] = jnp.exp(jnp.where(causal, a_mat[:, lane:lane + 1] + cmat_t[lane:lane + 1, :], NEG_INF))
            wi_s[g] = jnp.broadcast_to(w_inter[:, lane:lane + 1], (length, DV_A))
            em_s[g] = jnp.broadcast_to(e_m[:, lane:lane + 1], (length, DV_A))
            kwt_s[g] = (kh.astype(F32) * scale * w_k[:, lane:lane + 1]).T.astype(BF16)
            dc_s[g] = jnp.broadcast_to(decay[:, lane:lane + 1], (1, DV_A))
    qb = q_s[...]
    qk = _bdot_nt(qb, k_s[...])
    c_old = c_in[...].reshape(groups, DK_A, DV_A)
    nb_old = nb_in[...]
    qcx = _bdot(qb, jnp.concatenate([c_old, nb_old], axis=-1).astype(BF16))
    sw = (qk * scale * e_s[...]).astype(BF16)
    vx = vx_s[...]
    svx = _bdot(sw, vx)
    kvx = _bdot(kwt_s[...], vx)
    wi = wi_s[...]
    num = svx[:, :, 0:DV_A] + wi * qcx[:, :, 0:DV_A]
    den = svx[:, :, DV_A:2 * DV_A] + wi * qcx[:, :, DV_A:2 * DV_A]
    hh = num / jnp.maximum(jnp.abs(den), em_s[...])
    dc = dc_s[...]
    c_o[...] = (dc * c_old + kvx[:, :, 0:DV_A]).reshape(c_o.shape)
    nb_o[...] = dc * nb_old + kvx[:, :, DV_A:2 * DV_A]
    hn = _rms(hh) * nw_ref[...]
    for j, (_, _, _, o_ref, _, rows, write_hg) in enumerate(seqs):
        for h in range(H_A):
            cols = slice(h * DV_A, (h + 1) * DV_A)
            gate = _sigmoid(o_ref[rows, cols].astype(F32))
            write_hg(cols, gate * hn[j * H_A + h])


def _n_to_lanes(n_row):
    return jnp.broadcast_to(n_row, (LANES, DK_A)).T


def _n_from_lanes(nb):
    return nb.T[0:1, :]


def _mlstm_scratch(groups, length):
    return [pltpu.VMEM((groups, length, DK_A), BF16),
            pltpu.VMEM((groups, length, DK_A), BF16),
            pltpu.VMEM((groups, length, 2 * DV_A), BF16),
            pltpu.VMEM((groups, length, length), F32),
            pltpu.VMEM((groups, DK_A, length), BF16),
            pltpu.VMEM((groups, length, DV_A), F32),
            pltpu.VMEM((groups, length, DV_A), F32),
            pltpu.VMEM((groups, 1, DV_A), F32)]


def _mlstm_prompt_kernel(*refs, nseq):
    ins = refs[:5 * nseq]
    gb_ref, nw_ref, hg_ref, c_out, n_out, m_out, c_s, n_s, m_s = refs[5 * nseq:5 * nseq + 9]
    scratch = refs[5 * nseq + 9:]
    t = pl.program_id(1)

    @pl.when(t == 0)
    def _():
        c_s[...] = jnp.zeros_like(c_s)
        n_s[...] = jnp.zeros_like(n_s)
        m_s[...] = jnp.zeros_like(m_s)

    def chunk(c, carry):
        rows = pl.ds(pl.multiple_of(c * CHUNK, CHUNK), CHUNK)
        seqs = []
        for j in range(nseq):
            def write_hg(cols, val, j=j):
                hg_ref[j, rows, cols] = val.astype(hg_ref.dtype)
            seqs.append(tuple(ins[i * nseq + j] for i in range(5)) + (rows, write_hg))
        state = (c_s, n_s, m_s)
        _mlstm_chunk(seqs, gb_ref, nw_ref, state, state, scratch, CHUNK, CHUNK)
        return carry

    lax.fori_loop(0, hg_ref.shape[1] // CHUNK, chunk, 0)

    @pl.when(t == pl.num_programs(1) - 1)
    def _():
        c_out[...] = c_s[...].reshape(c_out.shape)
        for j in range(nseq):
            for h in range(H_A):
                n_out[j, h:h + 1, :] = _n_from_lanes(n_s[j * H_A + h])
        m_out[...] = m_s[...]


def _mlstm_prompt(q, k, v, o, g, gb, nw, batch, seq):
    tb = min(MLSTM_ROWS, seq)
    nt = seq // tb
    nseq = MLSTM_SEQS if batch % MLSTM_SEQS == 0 else 1
    groups = nseq * H_A
    const = lambda b, t: (0, 0)

    def rows_of(j):
        return lambda b, t: ((b * nseq + j) * nt + t, 0)

    in_specs, args = [], []
    for arr, width in ((q, QK_A), (k, QK_A), (v, VW_A), (o, VW_A), (g, LANES)):
        for j in range(nseq):
            in_specs.append(pl.BlockSpec((tb, width), rows_of(j)))
            args.append(arr)
    in_specs += [pl.BlockSpec((1, LANES), const), pl.BlockSpec((groups, 1, DV_A), lambda b, t: (0, 0, 0))]
    args += [gb, jnp.tile(nw.reshape(H_A, 1, DV_A), (nseq, 1, 1))]
    hg, c, n, m = pl.pallas_call(
        functools.partial(_mlstm_prompt_kernel, nseq=nseq),
        grid=(batch // nseq, nt),
        in_specs=in_specs,
        out_specs=[pl.BlockSpec((nseq, tb, VW_A), lambda b, t: (b, t, 0)),
                   pl.BlockSpec((nseq, H_A, DK_A, DV_A), lambda b, t: (b, 0, 0, 0)),
                   pl.BlockSpec((nseq, H_A, DK_A), lambda b, t: (b, 0, 0)),
                   pl.BlockSpec((nseq, 1, LANES), lambda b, t: (b, 0, 0))],
        out_shape=[jax.ShapeDtypeStruct((batch, seq, VW_A), BF16),
                   jax.ShapeDtypeStruct((batch, H_A, DK_A, DV_A), F32),
                   jax.ShapeDtypeStruct((batch, H_A, DK_A), F32),
                   jax.ShapeDtypeStruct((batch, 1, LANES), F32)],
        scratch_shapes=[pltpu.VMEM((groups, DK_A, DV_A), F32),
                        pltpu.VMEM((groups, DK_A, LANES), F32),
                        pltpu.VMEM((nseq, 1, LANES), F32)]
        + _mlstm_scratch(groups, CHUNK),
        compiler_params=_params("parallel", "arbitrary"),
        name="mlstm_prompt",
    )(*args)
    return hg.reshape(batch * seq, VW_A), c, n, m


def _mlstm_sample_kernel(q_ref, k_ref, v_ref, o_ref, g_ref, c0_ref, n0_ref, m0_ref, gb_ref, nw_ref, carry_ref,
                         hg_ref, c_out, n_out, m_out, nb_s, *scratch, nseq, length, valid):
    del carry_ref
    seqs = []
    for j in range(nseq):
        rows = slice(j * length, (j + 1) * length)

        def write_hg(cols, val, rows=rows):
            hg_ref[rows, cols] = val.astype(hg_ref.dtype)
        seqs.append((q_ref, k_ref, v_ref, o_ref, g_ref, rows, write_hg))
        for h in range(H_A):
            nb_s[j * H_A + h] = _n_to_lanes(n0_ref[j, h:h + 1, :])
    _mlstm_chunk(seqs, gb_ref, nw_ref, (c0_ref, nb_s, m0_ref), (c_out, nb_s, m_out), scratch, length, valid)
    for j in range(nseq):
        for h in range(H_A):
            n_out[j, h:h + 1, :] = _n_from_lanes(nb_s[j * H_A + h])


def _mlstm_sample(q, k, v, o, g, c_all, layer, c_carry, n0, m0, gb, nw, length, valid):
    batch = n0.shape[0]
    nseq = MLSTM_SAMPLE_SEQS if batch % MLSTM_SAMPLE_SEQS == 0 else 1
    groups = nseq * H_A
    rows = nseq * length
    row = lambda b: (b, 0)
    b3 = lambda b: (b, 0, 0)
    c_spec = pl.BlockSpec((None, nseq, H_A, DK_A, DV_A), lambda b: (layer, b, 0, 0, 0))
    args = [q, k, v, o, g, c_all, n0, m0, gb, jnp.tile(nw.reshape(H_A, 1, DV_A), (nseq, 1, 1))]
    in_specs = [pl.BlockSpec((rows, QK_A), row), pl.BlockSpec((rows, QK_A), row),
                pl.BlockSpec((rows, VW_A), row), pl.BlockSpec((rows, VW_A), row),
                pl.BlockSpec((rows, LANES), row),
                c_spec,
                pl.BlockSpec((nseq, H_A, DK_A), b3),
                pl.BlockSpec((nseq, 1, LANES), b3),
                pl.BlockSpec((1, LANES), lambda b: (0, 0)),
                pl.BlockSpec((groups, 1, DV_A), lambda b: (0, 0, 0))]
    aliases = {len(args): 1}
    args.append(c_carry)
    in_specs.append(pl.BlockSpec(memory_space=pl.ANY))
    return pl.pallas_call(
        functools.partial(_mlstm_sample_kernel, nseq=nseq, length=length, valid=valid),
        grid=(batch // nseq,),
        in_specs=in_specs,
        out_specs=[pl.BlockSpec((rows, VW_A), row),
                   c_spec,
                   pl.BlockSpec((nseq, H_A, DK_A), b3),
                   pl.BlockSpec((nseq, 1, LANES), b3)],
        out_shape=[jax.ShapeDtypeStruct((batch * length, VW_A), BF16),
                   jax.ShapeDtypeStruct(c_all.shape, F32),
                   jax.ShapeDtypeStruct((batch, H_A, DK_A), F32),
                   jax.ShapeDtypeStruct((batch, 1, LANES), F32)],
        scratch_shapes=[pltpu.VMEM((groups, DK_A, LANES), F32)] + _mlstm_scratch(groups, length),
        input_output_aliases=aliases,
        compiler_params=_params("arbitrary"),
        name="mlstm_sample",
    )(*args)


def _gdn_gates(ba, alog, dtb):
    beta = _sigmoid(ba)
    g = -jnp.exp(alog) * jax.nn.softplus(ba + dtb)
    return beta, g


def _bdot(a, b):
    return jnp.einsum('hmk,hkn->hmn', a, b, preferred_element_type=F32)


def _bdot_nt(a, b):
    return jnp.einsum('hmk,hnk->hmn', a, b, preferred_element_type=F32)


def _unit_lower_solve(nmat, x):
    size = nmat.shape[-1]
    span = 1
    while span < size:
        nh = nmat.astype(BF16)
        xh, xl = _split2(x)
        x = x + (_bdot(nh, xh) + _bdot(nh, xl))
        span *= 2
        if span < size:
            nmat = _bdot(nh, nh)
    return x


def _gdn_prompt_kernel(qkv_ref, z_ref, ba_ref, cw_ref, dtb_ref, alog_ref, nw_ref,
                       o_ref, s_out, conv_out,
                       xpad_s, s_s, q_s, k_s, x_s, n_s, attn_s, wq_s, kdt_s, gl_s):
    t = pl.program_id(1)
    tb = qkv_ref.shape[0]
    pad = SUBLANES

    @pl.when(t == 0)
    def _():
        s_s[...] = jnp.zeros_like(s_s)
        xpad_s[0:pad, :] = jnp.zeros((pad, CONV_DIM), F32)

    xpad_s[pad:pad + tb, :] = qkv_ref[...].astype(F32)

    tril, strict, _ = _tri_masks(CHUNK)
    tri_bf = tril.astype(BF16)
    scale = DK_B ** -0.5
    rep = HV_B // HK_B

    def chunk(c, carry):
        r0 = pl.multiple_of(c * CHUNK, CHUNK)
        rows = pl.ds(r0, CHUNK)

        def conv_silu(col0):
            y = xpad_s[pl.ds(r0, CHUNK + pad), col0:col0 + LANES]
            cw = cw_ref[:, col0:col0 + LANES]
            acc = y[pad - 3:pad - 3 + CHUNK] * cw[0:1]
            for j in range(1, CONV_W):
                acc = acc + y[pad - 3 + j:pad - 3 + j + CHUNK] * cw[j:j + 1]
            return acc * _sigmoid(acc)

        beta_all, g_all = _gdn_gates(ba_ref[rows, :], alog_ref[...], dtb_ref[...])
        gc = _cumsum_rows(tri_bf, g_all)
        gc_t = gc.T
        g_last = gc[CHUNK - 1:CHUNK, :]
        eg_all = jnp.exp(gc)
        egd_all = jnp.exp(g_last - gc)
        gl_all = jnp.exp(g_last)

        for kh in range(HK_B):
            q_s[kh] = _l2norm(conv_silu(kh * DK_B)) * scale
            k_s[kh] = _l2norm(conv_silu(QK_B + kh * DK_B))
        kb = k_s[...].astype(BF16)
        kk = _bdot_nt(kb, kb)
        qk = _bdot_nt(q_s[...].astype(BF16), kb)

        for hv in range(HV_B):
            kh = hv // rep
            gl = HV_B + hv
            v = conv_silu(2 * QK_B + hv * DV_B)
            k = k_s[kh]
            beta = beta_all[:, hv:hv + 1]
            eg = eg_all[:, gl:gl + 1]
            decay = jnp.exp(jnp.where(tril, gc[:, gl:gl + 1] - gc_t[gl:gl + 1, :], NEG_INF))
            n_s[hv] = jnp.where(strict, -(kk[kh] * beta) * decay, 0.0)
            x_s[hv, :, 0:DV_B] = v * beta
            x_s[hv, :, DV_B:2 * DV_B] = k * (beta * eg)
            attn_s[hv] = (qk[kh] * decay).astype(BF16)
            wq_s[hv, CHUNK:2 * CHUNK, :] = (q_s[kh] * eg).astype(BF16)
            kdt_s[hv] = (k * egd_all[:, gl:gl + 1]).T.astype(BF16)
            gl_s[hv] = jnp.broadcast_to(gl_all[:, gl:gl + 1], (1, LANES))

        sol = _unit_lower_solve(n_s[...], x_s[...])
        u = sol[:, :, 0:DV_B]
        wq_s[:, 0:CHUNK, :] = sol[:, :, DV_B:2 * DV_B].astype(BF16)

        s = s_s[...]
        ws = _bdot(wq_s[...], s.astype(BF16))
        vb = (u - ws[:, 0:CHUNK, :]).astype(BF16)
        o = ws[:, CHUNK:2 * CHUNK, :] + _bdot(attn_s[...], vb)
        s_s[...] = gl_s[...] * s + _bdot(kdt_s[...], vb)
        for hv in range(HV_B):
            cols = slice(hv * DV_B, (hv + 1) * DV_B)
            zz = z_ref[rows, cols].astype(F32)
            o_ref[rows, cols] = (_rms(o[hv]) * nw_ref[:, cols] * (zz * _sigmoid(zz))).astype(o_ref.dtype)
        return carry

    lax.fori_loop(0, tb // CHUNK, chunk, 0)

    xpad_s[0:pad, :] = xpad_s[tb:tb + pad, :]

    @pl.when(t == pl.num_programs(1) - 1)
    def _():
        s_out[0] = s_s[...]
        conv_out[0] = xpad_s[tb + pad - (CONV_W - 1):tb + pad, :]


def _gdn_prompt(qkv, z, ba, cw, dtb, alog, nw, batch, seq):
    tb = min(GDN_ROWS, seq)
    nt = seq // tb
    row = lambda b, t: (b * nt + t, 0)
    const = lambda b, t: (0, 0)
    return pl.pallas_call(
        _gdn_prompt_kernel,
        grid=(batch, nt),
        in_specs=[pl.BlockSpec((tb, CONV_DIM), row), pl.BlockSpec((tb, VW_B), row),
                  pl.BlockSpec((tb, LANES), row),
                  pl.BlockSpec((CONV_W, CONV_DIM), const),
                  pl.BlockSpec((1, LANES), const), pl.BlockSpec((1, LANES), const),
                  pl.BlockSpec((1, VW_B), const)],
        out_specs=[pl.BlockSpec((tb, VW_B), row),
                   pl.BlockSpec((1, HV_B, DK_B, DV_B), lambda b, t: (b, 0, 0, 0)),
                   pl.BlockSpec((1, CONV_W - 1, CONV_DIM), lambda b, t: (b, 0, 0))],
        out_shape=[jax.ShapeDtypeStruct((batch * seq, VW_B), BF16),
                   jax.ShapeDtypeStruct((batch, HV_B, DK_B, DV_B), F32),
                   jax.ShapeDtypeStruct((batch, CONV_W - 1, CONV_DIM), F32)],
        scratch_shapes=[pltpu.VMEM((tb + SUBLANES, CONV_DIM), F32),
                        pltpu.VMEM((HV_B, DK_B, DV_B), F32),
                        pltpu.VMEM((HK_B, CHUNK, DK_B), F32),
                        pltpu.VMEM((HK_B, CHUNK, DK_B), F32),
                        pltpu.VMEM((HV_B, CHUNK, 2 * DV_B), F32),
                        pltpu.VMEM((HV_B, CHUNK, CHUNK), F32),
                        pltpu.VMEM((HV_B, CHUNK, CHUNK), BF16),
                        pltpu.VMEM((HV_B, 2 * CHUNK, DK_B), BF16),
                        pltpu.VMEM((HV_B, DK_B, CHUNK), BF16),
                        pltpu.VMEM((HV_B, 1, LANES), F32)],
        compiler_params=_params("parallel", "arbitrary"),
        name="gdn_prompt",
    )(qkv, z, ba, cw, dtb, alog, nw)


def _gdn_sample_kernel(xp_ref, z_ref, ba_ref, s0_ref, cw_ref, dtb_ref, alog_ref, nw_ref, carry_ref,
                       o_ref, s_out, kq_s, o_s, *, steps):
    del carry_ref
    xp = xp_ref[0]
    cw = cw_ref[...]
    acc = xp[0:steps] * cw[0:1]
    for j in range(1, CONV_W):
        acc = acc + xp[j:j + steps] * cw[j:j + 1]
    c = acc * _sigmoid(acc)
    scale = DK_B ** -0.5
    for kh in range(HK_B):
        cols = slice(kh * DK_B, (kh + 1) * DK_B)
        kq_s[0:steps, cols] = _l2norm(c[:, QK_B + kh * DK_B:QK_B + (kh + 1) * DK_B])
        kq_s[steps:2 * steps, cols] = _l2norm(c[:, cols]) * scale
    beta_all, g_all = _gdn_gates(ba_ref[0], alog_ref[...], dtb_ref[...])
    eg_all = jnp.exp(g_all)
    rep = HV_B // HK_B
    for kh in range(HK_B):
        kq_t = kq_s[:, kh * DK_B:(kh + 1) * DK_B].T
        for j in range(rep):
            hv = kh * rep + j
            s = s0_ref[0, hv]
            for t in range(steps):
                k_col = kq_t[:, t:t + 1]
                q_col = kq_t[:, steps + t:steps + t + 1]
                eg = eg_all[t:t + 1, HV_B + hv:HV_B + hv + 1]
                beta = beta_all[t:t + 1, hv:hv + 1]
                v_row = c[t:t + 1, 2 * QK_B + hv * DV_B:2 * QK_B + (hv + 1) * DV_B]
                sd = eg * s
                pred = jnp.sum(sd * k_col, axis=0, keepdims=True)
                s = sd + k_col * (beta * (v_row - pred))
                o_s[t:t + 1, hv * DV_B:(hv + 1) * DV_B] = jnp.sum(s * q_col, axis=0, keepdims=True)
            s_out[0, hv] = s
    for hv in range(HV_B):
        cols = slice(hv * DV_B, (hv + 1) * DV_B)
        zz = z_ref[0, :, cols]
        o_ref[0, :, cols] = _rms(o_s[0:steps, cols]) * nw_ref[:, cols] * (zz * _sigmoid(zz))


def _gdn_sample(xp, z, ba, s_all, layer, s_carry, cw, dtb, alog, nw):
    batch, steps = z.shape[0], z.shape[1]
    b3 = lambda b: (b, 0, 0)
    const = lambda b: (0, 0)
    s_spec = pl.BlockSpec((None, 1, HV_B, DK_B, DV_B), lambda b: (layer, b, 0, 0, 0))
    args = [xp, z, ba, s_all, cw, dtb, alog, nw]
    in_specs = [pl.BlockSpec((1,) + xp.shape[1:], b3),
                pl.BlockSpec((1, steps, VW_B), b3),
                pl.BlockSpec((1, steps, LANES), b3),
                s_spec,
                pl.BlockSpec((CONV_W, CONV_DIM), const),
                pl.BlockSpec((1, LANES), const), pl.BlockSpec((1, LANES), const),
                pl.BlockSpec((1, VW_B), const)]
    aliases = {len(args): 1}
    args.append(s_carry)
    in_specs.append(pl.BlockSpec(memory_space=pl.ANY))
    return pl.pallas_call(
        functools.partial(_gdn_sample_kernel, steps=steps),
        grid=(batch,),
        in_specs=in_specs,
        out_specs=[pl.BlockSpec((1, steps, VW_B), b3), s_spec],
        out_shape=[jax.ShapeDtypeStruct((batch, steps, VW_B), F32),
                   jax.ShapeDtypeStruct(s_all.shape, F32)],
        scratch_shapes=[pltpu.VMEM((2 * steps, QK_B), F32),
                        pltpu.VMEM((SUBLANES, VW_B), F32)],
        input_output_aliases=aliases,
        compiler_params=_params("arbitrary"),
        name="gdn_sample",
    )(*args)


def _pad_cols(w, width):
    return jnp.pad(w, ((0, 0), (0, width - w.shape[1])))


def _lane_row(vec, offset):
    return jnp.zeros((1, LANES), F32).at[0, offset:offset + vec.shape[0]].set(vec.astype(F32))


MLSTM_SEGS = ((0, QK_A), (QK_A, QK_A), (2 * QK_A, VW_A), (2 * QK_A + VW_A, VW_A), (2 * QK_A + 2 * VW_A, LANES))
GDN_SEGS = ((0, CONV_DIM), (CONV_DIM, VW_B), (CONV_DIM + VW_B, LANES))


def _mlstm_layer(x, batch, seq, s_batch, s_seq, w_in, gate_b, norm_w, w_out, c_all, layer, c_carry, n0, m0,
                 ln_g, ln_b):
    w = _pad_cols(w_in, MLSTM_SEGS[-1][0] + LANES).astype(BF16)
    q, k, v, o, g = _proj(x, w, MLSTM_SEGS, (BF16, BF16, BF16, BF16, F32))
    gb = _lane_row(gate_b, 0)
    nw = norm_w.astype(F32).reshape(1, VW_A)
    hg_p, c_p, n_p, m_p = _mlstm_prompt(q[0], k[0], v[0], o[0], g[0], gb, nw, batch, seq)

    def padded(pair):
        a = pair[1].reshape(s_batch, s_seq, pair[1].shape[1])
        return jnp.pad(a, ((0, 0), (0, SAMPLE_CHUNK - s_seq), (0, 0))).reshape(s_batch * SAMPLE_CHUNK, a.shape[2])

    m0_lanes = jnp.pad(m0.astype(F32), ((0, 0), (H_A, LANES - 2 * H_A))).reshape(s_batch, 1, LANES)
    hg_s, c_s, n_s, m_s = _mlstm_sample(padded(q), padded(k), padded(v), padded(o), padded(g), c_all, layer,
                                        c_carry, n0, m0_lanes, gb, nw, SAMPLE_CHUNK, s_seq)
    hg_s = hg_s.reshape(s_batch, SAMPLE_CHUNK, VW_A)[:, :s_seq].reshape(s_batch * s_seq, VW_A)
    x = _outproj_ln((hg_p, hg_s), x, w_out.astype(BF16), ln_g.reshape(1, -1), ln_b.reshape(1, -1))
    return x, (c_p, n_p, m_p[:, 0, H_A:2 * H_A]), (c_s, n_s, m_s[:, 0, H_A:2 * H_A])


def _gdn_layer(x, batch, seq, s_batch, s_seq, w_in, conv_w, dt_bias, a_log, norm_w, w_out,
               s_all, layer, s_carry, conv0, ln_g, ln_b):
    w = _pad_cols(w_in, GDN_SEGS[-1][0] + LANES).astype(BF16)
    qkv, z, ba = _proj(x, w, GDN_SEGS, (BF16, BF16, F32))
    cw = conv_w.astype(F32)
    dtb = _lane_row(dt_bias, HV_B)
    alog = _lane_row(a_log, HV_B)
    nw = norm_w.astype(F32).reshape(1, VW_B)
    o_p, s_p, conv_p = _gdn_prompt(qkv[0], z[0], ba[0], cw, dtb, alog, nw, batch, seq)

    def tail(pair):
        return pair[1].astype(F32).reshape(s_batch, s_seq, pair[1].shape[1])

    xp = jnp.concatenate([conv0.astype(F32), tail(qkv)], axis=1)
    conv_s = xp[:, s_seq:]
    rows = xp.shape[1]
    xp = jnp.pad(xp, ((0, 0), (0, -rows % SUBLANES), (0, 0)))
    o_s, s_s = _gdn_sample(xp, tail(z), tail(ba), s_all, layer, s_carry, cw, dtb, alog, nw)
    o_s = o_s.reshape(s_batch * s_seq, VW_B).astype(BF16)
    x = _outproj_ln((o_p, o_s), x, w_out.astype(BF16), ln_g.reshape(1, -1), ln_b.reshape(1, -1))
    return x, (s_p, conv_p), (s_s, conv_s)


def kernel(x_prompt, x_sample, state_mlstm_C, state_mlstm_n, state_mlstm_m, state_gdn_S, state_gdn_conv,
           a_w_in, a_gate_b, a_norm_w, a_w_out, b_w_in, b_conv_w, b_dt_bias, b_a_log, b_norm_w, b_w_out,
           mlp_w1, mlp_w2, ln1_g, ln1_b, ln2_g, ln2_b):
    batch, seq, d = x_prompt.shape
    s_batch, s_seq, _ = x_sample.shape
    n_prompt = batch * seq
    n_sample = s_batch * s_seq
    assert n_prompt % ROW_TILE == 0 and n_sample == ROW_TILE and seq % CHUNK == 0 and s_seq <= SAMPLE_CHUNK
    x = (x_prompt.reshape(n_prompt, d), x_sample.reshape(n_sample, d))
    p_a, s_a, p_b, s_b = [], [], [], []
    c_carry = jnp.zeros(state_mlstm_C.shape, F32)
    s_carry = jnp.zeros(state_gdn_S.shape, F32)
    for layer in range(DEPTH):
        j = layer // 2
        if layer % 2 == 0:
            x, p_state, s_state = _mlstm_layer(
                x, batch, seq, s_batch, s_seq, a_w_in[j], a_gate_b[j], a_norm_w[j], a_w_out[j],
                state_mlstm_C, j, c_carry, state_mlstm_n[j], state_mlstm_m[j], ln1_g[layer], ln1_b[layer])
            c_carry = s_state[0]
            p_a.append(p_state)
            s_a.append(s_state)
        else:
            x, p_state, s_state = _gdn_layer(
                x, batch, seq, s_batch, s_seq, b_w_in[j], b_conv_w[j], b_dt_bias[j], b_a_log[j],
                b_norm_w[j], b_w_out[j], state_gdn_S, j, s_carry, state_gdn_conv[j], ln1_g[layer], ln1_b[layer])
            s_carry = s_state[0]
            p_b.append(p_state)
            s_b.append(s_state)
        x = _mlp_ln(x, mlp_w1[layer].astype(BF16), mlp_w2[layer].astype(BF16),
                    ln2_g[layer].reshape(1, -1), ln2_b[layer].reshape(1, -1))
    y_prompt = x[0].reshape(batch, seq, d)
    y_sample = x[1].reshape(s_batch, s_seq, d)

    def stack(states, i):
        return jnp.stack([s[i] for s in states])

    return (y_prompt, y_sample,
            stack(p_a, 0), stack(p_a, 1), stack(p_a, 2), stack(p_b, 0), stack(p_b, 1),
            c_carry, stack(s_a, 1), stack(s_a, 2), s_carry, stack(s_b, 1))
```

```python
import functools

import jax
import jax.numpy as jnp
from jax import lax
from jax.experimental import pallas as pl
from jax.experimental.pallas import tpu as pltpu

F32 = jnp.float32
BF16 = jnp.bfloat16

D_MODEL = 1024
DEPTH = 4
H_A = 8
DV_A = 128
DK_A = 64
QK_A = H_A * DK_A
VW_A = H_A * DV_A
GATE_CAP = 15.0
HK_B = 8
HV_B = 16
DK_B = 128
DV_B = 128
QK_B = HK_B * DK_B
VW_B = HV_B * DV_B
CONV_W = 4
CONV_DIM = 2 * QK_B + VW_B
CHUNK = 64
D_FF = 4 * D_MODEL
ALPHA = (2.0 * DEPTH) ** 0.25
LN_EPS = 1e-5
RMS_EPS = 1e-6

LANES = 128
SUBLANES = 8
VMEM_LIMIT = 56 * 1024 * 1024
ROW_TILE = 512
FF_TILE = 1024
MLSTM_ROWS = 512
MLSTM_SEQS = 1
MLSTM_SAMPLE_SEQS = 4
SAMPLE_CHUNK = 16
GDN_ROWS = 256

NEG_INF = float("-inf")


def _params(*sem):
    return pltpu.CompilerParams(dimension_semantics=sem, vmem_limit_bytes=VMEM_LIMIT)


def _dot(a, b):
    return jnp.dot(a, b, preferred_element_type=F32)


def _dot_nt(a, b):
    return lax.dot_general(a, b, (((1,), (1,)), ((), ())), preferred_element_type=F32)


def _dot_tn(a, b):
    return lax.dot_general(a, b, (((0,), (0,)), ((), ())), preferred_element_type=F32)


def _split2(x):
    hi = x.astype(BF16)
    lo = (x - hi.astype(F32)).astype(BF16)
    return hi, lo


def _dot_split(a, b):
    ah, al = _split2(a)
    bh, bl = _split2(b)
    return _dot(ah, bh) + (_dot(ah, bl) + _dot(al, bh))


def _cumsum_rows(tri_bf, x):
    x1 = x.astype(BF16)
    r1 = x - x1.astype(F32)
    x2 = r1.astype(BF16)
    x3 = (r1 - x2.astype(F32)).astype(BF16)
    return _dot(tri_bf, x1) + (_dot(tri_bf, x2) + _dot(tri_bf, x3))


def _layer_norm(r, g, b):
    mu = jnp.mean(r, axis=-1, keepdims=True)
    d = r - mu
    var = jnp.mean(d * d, axis=-1, keepdims=True)
    return d * lax.rsqrt(var + LN_EPS) * g + b


def _rms(h):
    return h * lax.rsqrt(jnp.mean(h * h, axis=-1, keepdims=True) + RMS_EPS)


def _l2norm(x):
    return x * lax.rsqrt(jnp.sum(x * x, axis=-1, keepdims=True) + RMS_EPS)


def _sigmoid(x):
    return jax.nn.sigmoid(x)


def _rows(x):
    return x[0].shape[0] + x[1].shape[0] if isinstance(x, tuple) else x.shape[0]


def _row_args(x):
    return list(x) if isinstance(x, tuple) else [x]


def _row_specs(x):
    if isinstance(x, tuple):
        main, tail = x
        last = main.shape[0] // ROW_TILE - 1
        return [pl.BlockSpec((ROW_TILE, main.shape[1]), lambda i: (jnp.minimum(i, last), 0)),
                pl.BlockSpec((ROW_TILE, tail.shape[1]), lambda i: (0, 0))]
    return [pl.BlockSpec((ROW_TILE, x.shape[1]), lambda i: (i, 0))]


def _row_load(refs):
    if len(refs) == 1:
        return refs[0][...]
    on_tail = pl.program_id(0) == pl.num_programs(0) - 1
    return jnp.where(on_tail, refs[1][...], refs[0][...])


def _proj_kernel(*refs, segs, nx):
    xb = _row_load(refs[:nx]).astype(BF16)
    w_ref = refs[nx]
    out_refs = refs[nx + 1:]
    for (start, width), o_ref in zip(segs, out_refs):
        for c0 in range(0, width, FF_TILE):
            cw = min(FF_TILE, width - c0)
            y = _dot(xb, w_ref[:, start + c0:start + c0 + cw])
            o_ref[:, c0:c0 + cw] = y.astype(o_ref.dtype)


def _proj(x, w, segs, dtypes):
    n = _rows(x)
    args = _row_args(x)
    return pl.pallas_call(
        functools.partial(_proj_kernel, segs=segs, nx=len(args)),
        grid=(n // ROW_TILE,),
        in_specs=_row_specs(x) + [pl.BlockSpec(w.shape, lambda i: (0, 0))],
        out_specs=[pl.BlockSpec((ROW_TILE, wd), lambda i: (i, 0)) for _, wd in segs],
        out_shape=[jax.ShapeDtypeStruct((n, wd), dt) for (_, wd), dt in zip(segs, dtypes)],
        compiler_params=_params("arbitrary"),
        name="proj",
    )(*args, w)


def _outproj_ln_kernel(*refs, nh, nx):
    w_ref, g_ref, b_ref, o_ref = refs[nh + nx:]
    y = _dot(_row_load(refs[:nh]), w_ref[...])
    o_ref[...] = _layer_norm(ALPHA * _row_load(refs[nh:nh + nx]) + y, g_ref[...], b_ref[...])


def _outproj_ln(h, x, w, g, b):
    kh, d = w.shape
    n = _rows(x)
    const = lambda i: (0, 0)
    h_args, x_args = _row_args(h), _row_args(x)
    return pl.pallas_call(
        functools.partial(_outproj_ln_kernel, nh=len(h_args), nx=len(x_args)),
        grid=(n // ROW_TILE,),
        in_specs=_row_specs(h) + _row_specs(x) + [pl.BlockSpec((kh, d), const),
                                                  pl.BlockSpec((1, d), const), pl.BlockSpec((1, d), const)],
        out_specs=pl.BlockSpec((ROW_TILE, d), lambda i: (i, 0)),
        out_shape=jax.ShapeDtypeStruct((n, d), F32),
        compiler_params=_params("arbitrary"),
        name="outproj_ln",
    )(*h_args, *x_args, w, g, b)


def _mlp_ln_kernel(x_ref, w1_ref, w2_ref, g_ref, b_ref, *out_and_scratch, split):
    acc_ref = out_and_scratch[-1]
    x = x_ref[...]
    xb = x.astype(BF16)
    dff = w1_ref.shape[1]
    for c0 in range(0, dff, FF_TILE):
        h = _dot(xb, w1_ref[:, c0:c0 + FF_TILE])
        h = jnp.square(jnp.maximum(h, 0.0)).astype(BF16)
        y = _dot(h, w2_ref[c0:c0 + FF_TILE, :])
        if c0 == 0:
            acc_ref[...] = y
        else:
            acc_ref[...] += y
    res = _layer_norm(ALPHA * x + acc_ref[...], g_ref[...], b_ref[...])
    if not split:
        out_and_scratch[0][...] = res
        return
    main_ref, tail_ref = out_and_scratch[:2]
    on_tail = pl.program_id(0) == pl.num_programs(0) - 1

    @pl.when(jnp.logical_not(on_tail))
    def _():
        main_ref[...] = res

    @pl.when(on_tail)
    def _():
        tail_ref[...] = res


def _mlp_ln(x, w1, w2, g, b, split_rows=None):
    n = x.shape[0]
    d, dff = w1.shape
    const = lambda i: (0, 0)
    if split_rows is None:
        out_specs = pl.BlockSpec((ROW_TILE, d), lambda i: (i, 0))
        out_shape = jax.ShapeDtypeStruct((n, d), F32)
    else:
        assert n - split_rows == ROW_TILE
        last = split_rows // ROW_TILE - 1
        out_specs = [pl.BlockSpec((ROW_TILE, d), lambda i: (jnp.minimum(i, last), 0)),
                     pl.BlockSpec((ROW_TILE, d), lambda i: (0, 0))]
        out_shape = [jax.ShapeDtypeStruct((split_rows, d), F32), jax.ShapeDtypeStruct((ROW_TILE, d), F32)]
    return pl.pallas_call(
        functools.partial(_mlp_ln_kernel, split=split_rows is not None),
        grid=(n // ROW_TILE,),
        in_specs=[pl.BlockSpec((ROW_TILE, d), lambda i: (i, 0)),
                  pl.BlockSpec((d, dff), const), pl.BlockSpec((dff, d), const),
                  pl.BlockSpec((1, d), const), pl.BlockSpec((1, d), const)],
        out_specs=out_specs,
        out_shape=out_shape,
        scratch_shapes=[pltpu.VMEM((ROW_TILE, d), F32)],
        compiler_params=_params("arbitrary"),
        name="mlp_ln",
    )(x, w1, w2, g, b)


def _tri_masks(n):
    row = lax.broadcasted_iota(jnp.int32, (n, n), 0)
    col = lax.broadcasted_iota(jnp.int32, (n, n), 1)
    return row >= col, row > col, row == col


def _mlstm_gates(g_raw, gb):
    cap = GATE_CAP * jnp.tanh((g_raw + gb) / GATE_CAP)
    return cap, jax.nn.log_sigmoid(cap)


def _mlstm_chunk(seqs, gb_ref, nw_ref, state_in, state_out, scratch, length, valid):
    c_in, nb_in, m_in = state_in
    c_o, nb_o, m_o = state_out
    q_s, k_s, vx_s, e_s, kwt_s, wi_s, em_s, dc_s = scratch
    nseq = len(seqs)
    groups = nseq * H_A
    causal, _, _ = _tri_masks(length)
    tri_bf = causal.astype(BF16)
    scale = DK_A ** -0.5
    row_id = lax.broadcasted_iota(jnp.int32, (length, LANES), 0)
    lane_id = lax.broadcasted_iota(jnp.int32, (length, LANES), 1)
    head_lanes = (lane_id >= H_A) & (lane_id < 2 * H_A)
    ones = jnp.ones((length, DV_A), BF16)

    for j, (q_ref, k_ref, v_ref, _, g_ref, rows, _) in enumerate(seqs):
        cap, lf = _mlstm_gates(g_ref[rows, :], gb_ref[...])
        i_sh = pltpu.roll(cap, H_A, axis=1)
        if valid < length:
            i_sh = jnp.where(row_id < valid, i_sh, NEG_INF)
            lf = jnp.where(row_id < valid, lf, 0.0)
        i_sh = jnp.where(head_lanes, i_sh, 0.0)
        bcum = jnp.where(head_lanes, _cumsum_rows(tri_bf, lf), 0.0)
        cmat = i_sh - bcum
        cmax = cmat
        span = 1
        while span < length:
            cmax = jnp.maximum(cmax, jnp.where(row_id >= span, pltpu.roll(cmax, span, axis=0), NEG_INF))
            span *= 2
        m_prev = m_in[j]
        m_t = bcum + jnp.maximum(m_prev, cmax)
        a_mat = bcum - m_t
        w_inter = jnp.exp(bcum + m_prev - m_t)
        e_m = jnp.exp(-m_t)
        m_new = m_t[length - 1:length, :]
        b_last = bcum[length - 1:length, :]
        w_k = jnp.exp(b_last - bcum + i_sh - m_new)
        decay = jnp.exp(b_last + m_prev - m_new)
        m_o[j] = m_new
        cmat_t = cmat.T
        for h in range(H_A):
            g = j * H_A + h
            lane = H_A + h
            q_s[g] = q_ref[rows, h * DK_A:(h + 1) * DK_A]
            kh = k_ref[rows, h * DK_A:(h + 1) * DK_A]
            k_s[g] = kh
            vx_s[g, :, 0:DV_A] = v_ref[rows, h * DV_A:(h + 1) * DV_A]
            vx_s[g, :, DV_A:2 * DV_A] = ones
            e_s[g] = jnp.exp(jnp.where(causal, a_mat[:, lane:lane + 1] + cmat_t[lane:lane + 1, :], NEG_INF))
            wi_s[g] = jnp.broadcast_to(w_inter[:, lane:lane + 1], (length, DV_A))
            em_s[g] = jnp.broadcast_to(e_m[:, lane:lane + 1], (length, DV_A))
            kwt_s[g] = (kh.astype(F32) * scale * w_k[:, lane:lane + 1]).T.astype(BF16)
            dc_s[g] = jnp.broadcast_to(decay[:, lane:lane + 1], (1, DV_A))
    qb = q_s[...]
    qk = _bdot_nt(qb, k_s[...])
    c_old = c_in[...].reshape(groups, DK_A, DV_A)
    nb_old = nb_in[...]
    qcx = _bdot(qb, jnp.concatenate([c_old, nb_old], axis=-1).astype(BF16))
    sw = (qk * scale * e_s[...]).astype(BF16)
    vx = vx_s[...]
    svx = _bdot(sw, vx)
    kvx = _bdot(kwt_s[...], vx)
    wi = wi_s[...]
    num = svx[:, :, 0:DV_A] + wi * qcx[:, :, 0:DV_A]
    den = svx[:, :, DV_A:2 * DV_A] + wi * qcx[:, :, DV_A:2 * DV_A]
    hh = num / jnp.maximum(jnp.abs(den), em_s[...])
    dc = dc_s[...]
    c_o[...] = (dc * c_old + kvx[:, :, 0:DV_A]).reshape(c_o.shape)
    nb_o[...] = dc * nb_old + kvx[:, :, DV_A:2 * DV_A]
    hn = _rms(hh) * nw_ref[...]
    for j, (_, _, _, o_ref, _, rows, write_hg) in enumerate(seqs):
        for h in range(H_A):
            cols = slice(h * DV_A, (h + 1) * DV_A)
            gate = _sigmoid(o_ref[rows, cols].astype(F32))
            write_hg(cols, gate * hn[j * H_A + h])


def _n_to_lanes(n_row):
    return jnp.broadcast_to(n_row, (LANES, DK_A)).T


def _n_from_lanes(nb):
    return nb.T[0:1, :]


def _mlstm_scratch(groups, length):
    return [pltpu.VMEM((groups, length, DK_A), BF16),
            pltpu.VMEM((groups, length, DK_A), BF16),
            pltpu.VMEM((groups, length, 2 * DV_A), BF16),
            pltpu.VMEM((groups, length, length), F32),
            pltpu.VMEM((groups, DK_A, length), BF16),
            pltpu.VMEM((groups, length, DV_A), F32),
            pltpu.VMEM((groups, length, DV_A), F32),
            pltpu.VMEM((groups, 1, DV_A), F32)]


def _mlstm_prompt_kernel(*refs, nseq):
    ins = refs[:5 * nseq]
    gb_ref, nw_ref, hg_ref, c_out, n_out, m_out, c_s, n_s, m_s = refs[5 * nseq:5 * nseq + 9]
    scratch = refs[5 * nseq + 9:]
    t = pl.program_id(1)

    @pl.when(t == 0)
    def _():
        c_s[...] = jnp.zeros_like(c_s)
        n_s[...] = jnp.zeros_like(n_s)
        m_s[...] = jnp.zeros_like(m_s)

    def chunk(c, carry):
        rows = pl.ds(pl.multiple_of(c * CHUNK, CHUNK), CHUNK)
        seqs = []
        for j in range(nseq):
            def write_hg(cols, val, j=j):
                hg_ref[j, rows, cols] = val.astype(hg_ref.dtype)
            seqs.append(tuple(ins[i * nseq + j] for i in range(5)) + (rows, write_hg))
        state = (c_s, n_s, m_s)
        _mlstm_chunk(seqs, gb_ref, nw_ref, state, state, scratch, CHUNK, CHUNK)
        return carry

    lax.fori_loop(0, hg_ref.shape[1] // CHUNK, chunk, 0)

    @pl.when(t == pl.num_programs(1) - 1)
    def _():
        c_out[...] = c_s[...].reshape(c_out.shape)
        for j in range(nseq):
            for h in range(H_A):
                n_out[j, h:h + 1, :] = _n_from_lanes(n_s[j * H_A + h])
        m_out[...] = m_s[...]


def _mlstm_prompt(q, k, v, o, g, gb, nw, batch, seq):
    tb = min(MLSTM_ROWS, seq)
    nt = seq // tb
    nseq = MLSTM_SEQS if batch % MLSTM_SEQS == 0 else 1
    groups = nseq * H_A
    const = lambda b, t: (0, 0)

    def rows_of(j):
        return lambda b, t: ((b * nseq + j) * nt + t, 0)

    in_specs, args = [], []
    for arr, width in ((q, QK_A), (k, QK_A), (v, VW_A), (o, VW_A), (g, LANES)):
        for j in range(nseq):
            in_specs.append(pl.BlockSpec((tb, width), rows_of(j)))
            args.append(arr)
    in_specs += [pl.BlockSpec((1, LANES), const), pl.BlockSpec((groups, 1, DV_A), lambda b, t: (0, 0, 0))]
    args += [gb, jnp.tile(nw.reshape(H_A, 1, DV_A), (nseq, 1, 1))]
    hg, c, n, m = pl.pallas_call(
        functools.partial(_mlstm_prompt_kernel, nseq=nseq),
        grid=(batch // nseq, nt),
        in_specs=in_specs,
        out_specs=[pl.BlockSpec((nseq, tb, VW_A), lambda b, t: (b, t, 0)),
                   pl.BlockSpec((nseq, H_A, DK_A, DV_A), lambda b, t: (b, 0, 0, 0)),
                   pl.BlockSpec((nseq, H_A, DK_A), lambda b, t: (b, 0, 0)),
                   pl.BlockSpec((nseq, 1, LANES), lambda b, t: (b, 0, 0))],
        out_shape=[jax.ShapeDtypeStruct((batch, seq, VW_A), BF16),
                   jax.ShapeDtypeStruct((batch, H_A, DK_A, DV_A), F32),
                   jax.ShapeDtypeStruct((batch, H_A, DK_A), F32),
                   jax.ShapeDtypeStruct((batch, 1, LANES), F32)],
        scratch_shapes=[pltpu.VMEM((groups, DK_A, DV_A), F32),
                        pltpu.VMEM((groups, DK_A, LANES), F32),
                        pltpu.VMEM((nseq, 1, LANES), F32)]
        + _mlstm_scratch(groups, CHUNK),
        compiler_params=_params("parallel", "arbitrary"),
        name="mlstm_prompt",
    )(*args)
    return hg.reshape(batch * seq, VW_A), c, n, m


def _mlstm_sample_kernel(q_ref, k_ref, v_ref, o_ref, g_ref, c0_ref, n0_ref, m0_ref, gb_ref, nw_ref, carry_ref,
                         hg_ref, c_out, n_out, m_out, nb_s, *scratch, nseq, length, valid):
    del carry_ref
    seqs = []
    for j in range(nseq):
        rows = slice(j * length, (j + 1) * length)

        def write_hg(cols, val, rows=rows):
            hg_ref[rows, cols] = val.astype(hg_ref.dtype)
        seqs.append((q_ref, k_ref, v_ref, o_ref, g_ref, rows, write_hg))
        for h in range(H_A):
            nb_s[j * H_A + h] = _n_to_lanes(n0_ref[j, h:h + 1, :])
    _mlstm_chunk(seqs, gb_ref, nw_ref, (c0_ref, nb_s, m0_ref), (c_out, nb_s, m_out), scratch, length, valid)
    for j in range(nseq):
        for h in range(H_A):
            n_out[j, h:h + 1, :] = _n_from_lanes(nb_s[j * H_A + h])


def _mlstm_sample(q, k, v, o, g, c_all, layer, c_carry, n0, m0, gb, nw, length, valid):
    batch = n0.shape[0]
    nseq = MLSTM_SAMPLE_SEQS if batch % MLSTM_SAMPLE_SEQS == 0 else 1
    groups = nseq * H_A
    rows = nseq * length
    row = lambda b: (b, 0)
    b3 = lambda b: (b, 0, 0)
    c_spec = pl.BlockSpec((None, nseq, H_A, DK_A, DV_A), lambda b: (layer, b, 0, 0, 0))
    args = [q, k, v, o, g, c_all, n0, m0, gb, jnp.tile(nw.reshape(H_A, 1, DV_A), (nseq, 1, 1))]
    in_specs = [pl.BlockSpec((rows, QK_A), row), pl.BlockSpec((rows, QK_A), row),
                pl.BlockSpec((rows, VW_A), row), pl.BlockSpec((rows, VW_A), row),
                pl.BlockSpec((rows, LANES), row),
                c_spec,
                pl.BlockSpec((nseq, H_A, DK_A), b3),
                pl.BlockSpec((nseq, 1, LANES), b3),
                pl.BlockSpec((1, LANES), lambda b: (0, 0)),
                pl.BlockSpec((groups, 1, DV_A), lambda b: (0, 0, 0))]
    aliases = {len(args): 1}
    args.append(c_carry)
    in_specs.append(pl.BlockSpec(memory_space=pl.ANY))
    return pl.pallas_call(
        functools.partial(_mlstm_sample_kernel, nseq=nseq, length=length, valid=valid),
        grid=(batch // nseq,),
        in_specs=in_specs,
        out_specs=[pl.BlockSpec((rows, VW_A), row),
                   c_spec,
                   pl.BlockSpec((nseq, H_A, DK_A), b3),
                   pl.BlockSpec((nseq, 1, LANES), b3)],
        out_shape=[jax.ShapeDtypeStruct((batch * length, VW_A), BF16),
                   jax.ShapeDtypeStruct(c_all.shape, F32),
                   jax.ShapeDtypeStruct((batch, H_A, DK_A), F32),
                   jax.ShapeDtypeStruct((batch, 1, LANES), F32)],
        scratch_shapes=[pltpu.VMEM((groups, DK_A, LANES), F32)] + _mlstm_scratch(groups, length),
        input_output_aliases=aliases,
        compiler_params=_params("arbitrary"),
        name="mlstm_sample",
    )(*args)


def _gdn_gates(ba, alog, dtb):
    beta = _sigmoid(ba)
    g = -jnp.exp(alog) * jax.nn.softplus(ba + dtb)
    return beta, g


def _bdot(a, b):
    return jnp.einsum('hmk,hkn->hmn', a, b, preferred_element_type=F32)


def _bdot_nt(a, b):
    return jnp.einsum('hmk,hnk->hmn', a, b, preferred_element_type=F32)


def _unit_lower_solve(nmat, x):
    size = nmat.shape[-1]
    span = 1
    while span < size:
        nh = nmat.astype(BF16)
        xh, xl = _split2(x)
        x = x + (_bdot(nh, xh) + _bdot(nh, xl))
        span *= 2
        if span < size:
            nmat = _bdot(nh, nh)
    return x


def _gdn_prompt_kernel(qkv_ref, z_ref, ba_ref, cw_ref, dtb_ref, alog_ref, nw_ref,
                       o_ref, s_out, conv_out,
                       xpad_s, s_s, q_s, k_s, x_s, n_s, attn_s, wq_s, kdt_s, gl_s):
    t = pl.program_id(1)
    tb = qkv_ref.shape[0]
    pad = SUBLANES

    @pl.when(t == 0)
    def _():
        s_s[...] = jnp.zeros_like(s_s)
        xpad_s[0:pad, :] = jnp.zeros((pad, CONV_DIM), F32)

    xpad_s[pad:pad + tb, :] = qkv_ref[...].astype(F32)

    tril, strict, _ = _tri_masks(CHUNK)
    tri_bf = tril.astype(BF16)
    scale = DK_B ** -0.5
    rep = HV_B // HK_B

    def chunk(c, carry):
        r0 = pl.multiple_of(c * CHUNK, CHUNK)
        rows = pl.ds(r0, CHUNK)

        def conv_silu(col0):
            y = xpad_s[pl.ds(r0, CHUNK + pad), col0:col0 + LANES]
            cw = cw_ref[:, col0:col0 + LANES]
            acc = y[pad - 3:pad - 3 + CHUNK] * cw[0:1]
            for j in range(1, CONV_W):
                acc = acc + y[pad - 3 + j:pad - 3 + j + CHUNK] * cw[j:j + 1]
            return acc * _sigmoid(acc)

        beta_all, g_all = _gdn_gates(ba_ref[rows, :], alog_ref[...], dtb_ref[...])
        gc = _cumsum_rows(tri_bf, g_all)
        gc_t = gc.T
        g_last = gc[CHUNK - 1:CHUNK, :]
        eg_all = jnp.exp(gc)
        egd_all = jnp.exp(g_last - gc)
        gl_all = jnp.exp(g_last)

        for kh in range(HK_B):
            q_s[kh] = _l2norm(conv_silu(kh * DK_B)) * scale
            k_s[kh] = _l2norm(conv_silu(QK_B + kh * DK_B))
        kb = k_s[...].astype(BF16)
        kk = _bdot_nt(kb, kb)
        qk = _bdot_nt(q_s[...].astype(BF16), kb)

        for hv in range(HV_B):
            kh = hv // rep
            gl = HV_B + hv
            v = conv_silu(2 * QK_B + hv * DV_B)
            k = k_s[kh]
            beta = beta_all[:, hv:hv + 1]
            eg = eg_all[:, gl:gl + 1]
            decay = jnp.exp(jnp.where(tril, gc[:, gl:gl + 1] - gc_t[gl:gl + 1, :], NEG_INF))
            n_s[hv] = jnp.where(strict, -(kk[kh] * beta) * decay, 0.0)
            x_s[hv, :, 0:DV_B] = v * beta
            x_s[hv, :, DV_B:2 * DV_B] = k * (beta * eg)
            attn_s[hv] = (qk[kh] * decay).astype(BF16)
            wq_s[hv, CHUNK:2 * CHUNK, :] = (q_s[kh] * eg).astype(BF16)
            kdt_s[hv] = (k * egd_all[:, gl:gl + 1]).T.astype(BF16)
            gl_s[hv] = jnp.broadcast_to(gl_all[:, gl:gl + 1], (1, LANES))

        sol = _unit_lower_solve(n_s[...], x_s[...])
        u = sol[:, :, 0:DV_B]
        wq_s[:, 0:CHUNK, :] = sol[:, :, DV_B:2 * DV_B].astype(BF16)

        s = s_s[...]
        ws = _bdot(wq_s[...], s.astype(BF16))
        vb = (u - ws[:, 0:CHUNK, :]).astype(BF16)
        o = ws[:, CHUNK:2 * CHUNK, :] + _bdot(attn_s[...], vb)
        s_s[...] = gl_s[...] * s + _bdot(kdt_s[...], vb)
        for hv in range(HV_B):
            cols = slice(hv * DV_B, (hv + 1) * DV_B)
            zz = z_ref[rows, cols].astype(F32)
            o_ref[rows, cols] = (_rms(o[hv]) * nw_ref[:, cols] * (zz * _sigmoid(zz))).astype(o_ref.dtype)
        return carry

    lax.fori_loop(0, tb // CHUNK, chunk, 0)

    xpad_s[0:pad, :] = xpad_s[tb:tb + pad, :]

    @pl.when(t == pl.num_programs(1) - 1)
    def _():
        s_out[0] = s_s[...]
        conv_out[0] = xpad_s[tb + pad - (CONV_W - 1):tb + pad, :]


def _gdn_prompt(qkv, z, ba, cw, dtb, alog, nw, batch, seq):
    tb = min(GDN_ROWS, seq)
    nt = seq // tb
    row = lambda b, t: (b * nt + t, 0)
    const = lambda b, t: (0, 0)
    return pl.pallas_call(
        _gdn_prompt_kernel,
        grid=(batch, nt),
        in_specs=[pl.BlockSpec((tb, CONV_DIM), row), pl.BlockSpec((tb, VW_B), row),
                  pl.BlockSpec((tb, LANES), row),
                  pl.BlockSpec((CONV_W, CONV_DIM), const),
                  pl.BlockSpec((1, LANES), const), pl.BlockSpec((1, LANES), const),
                  pl.BlockSpec((1, VW_B), const)],
        out_specs=[pl.BlockSpec((tb, VW_B), row),
                   pl.BlockSpec((1, HV_B, DK_B, DV_B), lambda b, t: (b, 0, 0, 0)),
                   pl.BlockSpec((1, CONV_W - 1, CONV_DIM), lambda b, t: (b, 0, 0))],
        out_shape=[jax.ShapeDtypeStruct((batch * seq, VW_B), BF16),
                   jax.ShapeDtypeStruct((batch, HV_B, DK_B, DV_B), F32),
                   jax.ShapeDtypeStruct((batch, CONV_W - 1, CONV_DIM), F32)],
        scratch_shapes=[pltpu.VMEM((tb + SUBLANES, CONV_DIM), F32),
                        pltpu.VMEM((HV_B, DK_B, DV_B), F32),
                        pltpu.VMEM((HK_B, CHUNK, DK_B), F32),
                        pltpu.VMEM((HK_B, CHUNK, DK_B), F32),
                        pltpu.VMEM((HV_B, CHUNK, 2 * DV_B), F32),
                        pltpu.VMEM((HV_B, CHUNK, CHUNK), F32),
                        pltpu.VMEM((HV_B, CHUNK, CHUNK), BF16),
                        pltpu.VMEM((HV_B, 2 * CHUNK, DK_B), BF16),
                        pltpu.VMEM((HV_B, DK_B, CHUNK), BF16),
                        pltpu.VMEM((HV_B, 1, LANES), F32)],
        compiler_params=_params("parallel", "arbitrary"),
        name="gdn_prompt",
    )(qkv, z, ba, cw, dtb, alog, nw)


def _gdn_sample_kernel(xp_ref, z_ref, ba_ref, s0_ref, cw_ref, dtb_ref, alog_ref, nw_ref, carry_ref,
                       o_ref, s_out, kq_s, o_s, *, steps):
    del carry_ref
    xp = xp_ref[0]
    cw = cw_ref[...]
    acc = xp[0:steps] * cw[0:1]
    for j in range(1, CONV_W):
        acc = acc + xp[j:j + steps] * cw[j:j + 1]
    c = acc * _sigmoid(acc)
    scale = DK_B ** -0.5
    for kh in range(HK_B):
        cols = slice(kh * DK_B, (kh + 1) * DK_B)
        kq_s[0:steps, cols] = _l2norm(c[:, QK_B + kh * DK_B:QK_B + (kh + 1) * DK_B])
        kq_s[steps:2 * steps, cols] = _l2norm(c[:, cols]) * scale
    beta_all, g_all = _gdn_gates(ba_ref[0], alog_ref[...], dtb_ref[...])
    eg_all = jnp.exp(g_all)
    rep = HV_B // HK_B
    for kh in range(HK_B):
        kq_t = kq_s[:, kh * DK_B:(kh + 1) * DK_B].T
        for j in range(rep):
            hv = kh * rep + j
            s = s0_ref[0, hv]
            for t in range(steps):
                k_col = kq_t[:, t:t + 1]
                q_col = kq_t[:, steps + t:steps + t + 1]
                eg = eg_all[t:t + 1, HV_B + hv:HV_B + hv + 1]
                beta = beta_all[t:t + 1, hv:hv + 1]
                v_row = c[t:t + 1, 2 * QK_B + hv * DV_B:2 * QK_B + (hv + 1) * DV_B]
                sd = eg * s
                pred = jnp.sum(sd * k_col, axis=0, keepdims=True)
                s = sd + k_col * (beta * (v_row - pred))
                o_s[t:t + 1, hv * DV_B:(hv + 1) * DV_B] = jnp.sum(s * q_col, axis=0, keepdims=True)
            s_out[0, hv] = s
    for hv in range(HV_B):
        cols = slice(hv * DV_B, (hv + 1) * DV_B)
        zz = z_ref[0, :, cols]
        o_ref[0, :, cols] = _rms(o_s[0:steps, cols]) * nw_ref[:, cols] * (zz * _sigmoid(zz))


def _gdn_sample(xp, z, ba, s_all, layer, s_carry, cw, dtb, alog, nw):
    batch, steps = z.shape[0], z.shape[1]
    b3 = lambda b: (b, 0, 0)
    const = lambda b: (0, 0)
    s_spec = pl.BlockSpec((None, 1, HV_B, DK_B, DV_B), lambda b: (layer, b, 0, 0, 0))
    args = [xp, z, ba, s_all, cw, dtb, alog, nw]
    in_specs = [pl.BlockSpec((1,) + xp.shape[1:], b3),
                pl.BlockSpec((1, steps, VW_B), b3),
                pl.BlockSpec((1, steps, LANES), b3),
                s_spec,
                pl.BlockSpec((CONV_W, CONV_DIM), const),
                pl.BlockSpec((1, LANES), const), pl.BlockSpec((1, LANES), const),
                pl.BlockSpec((1, VW_B), const)]
    aliases = {len(args): 1}
    args.append(s_carry)
    in_specs.append(pl.BlockSpec(memory_space=pl.ANY))
    return pl.pallas_call(
        functools.partial(_gdn_sample_kernel, steps=steps),
        grid=(batch,),
        in_specs=in_specs,
        out_specs=[pl.BlockSpec((1, steps, VW_B), b3), s_spec],
        out_shape=[jax.ShapeDtypeStruct((batch, steps, VW_B), F32),
                   jax.ShapeDtypeStruct(s_all.shape, F32)],
        scratch_shapes=[pltpu.VMEM((2 * steps, QK_B), F32),
                        pltpu.VMEM((SUBLANES, VW_B), F32)],
        input_output_aliases=aliases,
        compiler_params=_params("arbitrary"),
        name="gdn_sample",
    )(*args)


def _pad_cols(w, width):
    return jnp.pad(w, ((0, 0), (0, width - w.shape[1])))


def _lane_row(vec, offset):
    return jnp.zeros((1, LANES), F32).at[0, offset:offset + vec.shape[0]].set(vec.astype(F32))


MLSTM_SEGS = ((0, QK_A), (QK_A, QK_A), (2 * QK_A, VW_A), (2 * QK_A + VW_A, VW_A), (2 * QK_A + 2 * VW_A, LANES))
GDN_SEGS = ((0, CONV_DIM), (CONV_DIM, VW_B), (CONV_DIM + VW_B, LANES))


def _mlstm_layer(x, batch, seq, s_batch, s_seq, w_in, gate_b, norm_w, w_out, c_all, layer, c_carry, n0, m0,
                 ln_g, ln_b):
    w = _pad_cols(w_in, MLSTM_SEGS[-1][0] + LANES).astype(BF16)
    q, k, v, o, g = _proj(x, w, MLSTM_SEGS, (BF16, BF16, BF16, BF16, F32))
    gb = _lane_row(gate_b, 0)
    nw = norm_w.astype(F32).reshape(1, VW_A)
    hg_p, c_p, n_p, m_p = _mlstm_prompt(q, k, v, o, g, gb, nw, batch, seq)

    def padded(a):
        a = a[batch * seq:].reshape(s_batch, s_seq, a.shape[1])
        return jnp.pad(a, ((0, 0), (0, SAMPLE_CHUNK - s_seq), (0, 0))).reshape(s_batch * SAMPLE_CHUNK, a.shape[2])

    m0_lanes = jnp.pad(m0.astype(F32), ((0, 0), (H_A, LANES - 2 * H_A))).reshape(s_batch, 1, LANES)
    hg_s, c_s, n_s, m_s = _mlstm_sample(padded(q), padded(k), padded(v), padded(o), padded(g), c_all, layer,
                                        c_carry, n0, m0_lanes, gb, nw, SAMPLE_CHUNK, s_seq)
    hg_s = hg_s.reshape(s_batch, SAMPLE_CHUNK, VW_A)[:, :s_seq].reshape(s_batch * s_seq, VW_A)
    x = _outproj_ln((hg_p, hg_s), x, w_out.astype(BF16), ln_g.reshape(1, -1), ln_b.reshape(1, -1))
    return x, (c_p, n_p, m_p[:, 0, H_A:2 * H_A]), (c_s, n_s, m_s[:, 0, H_A:2 * H_A])


def _gdn_layer(x, batch, seq, s_batch, s_seq, w_in, conv_w, dt_bias, a_log, norm_w, w_out,
               s_all, layer, s_carry, conv0, ln_g, ln_b):
    w = _pad_cols(w_in, GDN_SEGS[-1][0] + LANES).astype(BF16)
    qkv, z, ba = _proj(x, w, GDN_SEGS, (BF16, BF16, F32))
    cw = conv_w.astype(F32)
    dtb = _lane_row(dt_bias, HV_B)
    alog = _lane_row(a_log, HV_B)
    nw = norm_w.astype(F32).reshape(1, VW_B)
    o_p, s_p, conv_p = _gdn_prompt(qkv, z, ba, cw, dtb, alog, nw, batch, seq)

    def tail(a):
        return a[batch * seq:].astype(F32).reshape(s_batch, s_seq, a.shape[1])

    xp = jnp.concatenate([conv0.astype(F32), tail(qkv)], axis=1)
    conv_s = xp[:, s_seq:]
    rows = xp.shape[1]
    xp = jnp.pad(xp, ((0, 0), (0, -rows % SUBLANES), (0, 0)))
    o_s, s_s = _gdn_sample(xp, tail(z), tail(ba), s_all, layer, s_carry, cw, dtb, alog, nw)
    o_s = o_s.reshape(s_batch * s_seq, VW_B).astype(BF16)
    x = _outproj_ln((o_p, o_s), x, w_out.astype(BF16), ln_g.reshape(1, -1), ln_b.reshape(1, -1))
    return x, (s_p, conv_p), (s_s, conv_s)


def kernel(x_prompt, x_sample, state_mlstm_C, state_mlstm_n, state_mlstm_m, state_gdn_S, state_gdn_conv,
           a_w_in, a_gate_b, a_norm_w, a_w_out, b_w_in, b_conv_w, b_dt_bias, b_a_log, b_norm_w, b_w_out,
           mlp_w1, mlp_w2, ln1_g, ln1_b, ln2_g, ln2_b):
    batch, seq, d = x_prompt.shape
    s_batch, s_seq, _ = x_sample.shape
    n_prompt = batch * seq
    n_sample = s_batch * s_seq
    assert n_prompt % ROW_TILE == 0 and n_sample == ROW_TILE and seq % CHUNK == 0 and s_seq <= SAMPLE_CHUNK
    x = (x_prompt.reshape(n_prompt, d), x_sample.reshape(n_sample, d))
    p_a, s_a, p_b, s_b = [], [], [], []
    c_carry = jnp.zeros(state_mlstm_C.shape, F32)
    s_carry = jnp.zeros(state_gdn_S.shape, F32)
    for layer in range(DEPTH):
        j = layer // 2
        if layer % 2 == 0:
            x, p_state, s_state = _mlstm_layer(
                x, batch, seq, s_batch, s_seq, a_w_in[j], a_gate_b[j], a_norm_w[j], a_w_out[j],
                state_mlstm_C, j, c_carry, state_mlstm_n[j], state_mlstm_m[j], ln1_g[layer], ln1_b[layer])
            c_carry = s_state[0]
            p_a.append(p_state)
            s_a.append(s_state)
        else:
            x, p_state, s_state = _gdn_layer(
                x, batch, seq, s_batch, s_seq, b_w_in[j], b_conv_w[j], b_dt_bias[j], b_a_log[j],
                b_norm_w[j], b_w_out[j], state_gdn_S, j, s_carry, state_gdn_conv[j], ln1_g[layer], ln1_b[layer])
            s_carry = s_state[0]
            p_b.append(p_state)
            s_b.append(s_state)
        x = _mlp_ln(x, mlp_w1[layer].astype(BF16), mlp_w2[layer].astype(BF16),
                    ln2_g[layer].reshape(1, -1), ln2_b[layer].reshape(1, -1),
                    split_rows=n_prompt if layer == DEPTH - 1 else None)
    y_prompt = x[0].reshape(batch, seq, d)
    y_sample = x[1].reshape(s_batch, s_seq, d)

    def stack(states, i):
        return jnp.stack([s[i] for s in states])

    return (y_prompt, y_sample,
            stack(p_a, 0), stack(p_a, 1), stack(p_a, 2), stack(p_b, 0), stack(p_b, 1),
            c_carry, stack(s_a, 1), stack(s_a, 2), s_carry, stack(s_b, 1))
```

```python
import functools

import jax
import jax.numpy as jnp
from jax import lax
from jax.experimental import pallas as pl
from jax.experimental.pallas import tpu as pltpu

F32 = jnp.float32
BF16 = jnp.bfloat16

D_MODEL = 1024
DEPTH = 4
H_A = 8
DV_A = 128
DK_A = 64
QK_A = H_A * DK_A
VW_A = H_A * DV_A
GATE_CAP = 15.0
HK_B = 8
HV_B = 16
DK_B = 128
DV_B = 128
QK_B = HK_B * DK_B
VW_B = HV_B * DV_B
CONV_W = 4
CONV_DIM = 2 * QK_B + VW_B
CHUNK = 64
D_FF = 4 * D_MODEL
ALPHA = (2.0 * DEPTH) ** 0.25
LN_EPS = 1e-5
RMS_EPS = 1e-6

LANES = 128
SUBLANES = 8
VMEM_LIMIT = 56 * 1024 * 1024
ROW_TILE = 512
FF_TILE = 1024
MLSTM_ROWS = 512
MLSTM_SEQS = 1
MLSTM_SAMPLE_SEQS = 4
GDN_SAMPLE_SEQS = 2
SAMPLE_CHUNK = 16
GDN_ROWS = 256

NEG_INF = float("-inf")


def _params(*sem):
    return pltpu.CompilerParams(dimension_semantics=sem, vmem_limit_bytes=VMEM_LIMIT)


def _dot(a, b):
    return jnp.dot(a, b, preferred_element_type=F32)


def _dot_nt(a, b):
    return lax.dot_general(a, b, (((1,), (1,)), ((), ())), preferred_element_type=F32)


def _dot_tn(a, b):
    return lax.dot_general(a, b, (((0,), (0,)), ((), ())), preferred_element_type=F32)


def _split2(x):
    hi = x.astype(BF16)
    lo = (x - hi.astype(F32)).astype(BF16)
    return hi, lo


def _dot_split(a, b):
    ah, al = _split2(a)
    bh, bl = _split2(b)
    return _dot(ah, bh) + (_dot(ah, bl) + _dot(al, bh))


def _cumsum_rows(tri_bf, x):
    x1 = x.astype(BF16)
    r1 = x - x1.astype(F32)
    x2 = r1.astype(BF16)
    x3 = (r1 - x2.astype(F32)).astype(BF16)
    return _dot(tri_bf, x1) + (_dot(tri_bf, x2) + _dot(tri_bf, x3))


def _layer_norm(r, g, b):
    mu = jnp.mean(r, axis=-1, keepdims=True)
    d = r - mu
    var = jnp.mean(d * d, axis=-1, keepdims=True)
    return d * lax.rsqrt(var + LN_EPS) * g + b


def _rms(h):
    return h * lax.rsqrt(jnp.mean(h * h, axis=-1, keepdims=True) + RMS_EPS)


def _l2norm(x):
    return x * lax.rsqrt(jnp.sum(x * x, axis=-1, keepdims=True) + RMS_EPS)


def _sigmoid(x):
    return 0.5 * jnp.tanh(0.5 * x) + 0.5


def _rows(x):
    return x[0].shape[0] + x[1].shape[0] if isinstance(x, tuple) else x.shape[0]


def _row_args(x):
    return list(x) if isinstance(x, tuple) else [x]


def _row_specs(x):
    if isinstance(x, tuple):
        main, tail = x
        last = main.shape[0] // ROW_TILE - 1
        return [pl.BlockSpec((ROW_TILE, main.shape[1]), lambda i: (jnp.minimum(i, last), 0)),
                pl.BlockSpec((ROW_TILE, tail.shape[1]), lambda i: (0, 0))]
    return [pl.BlockSpec((ROW_TILE, x.shape[1]), lambda i: (i, 0))]


def _row_load(refs):
    if len(refs) == 1:
        return refs[0][...]
    on_tail = pl.program_id(0) == pl.num_programs(0) - 1
    return jnp.where(on_tail, refs[1][...], refs[0][...])


def _proj_kernel(*refs, segs, nx):
    xb = _row_load(refs[:nx]).astype(BF16)
    w_ref = refs[nx]
    out_refs = refs[nx + 1:]
    for (start, width), o_ref in zip(segs, out_refs):
        for c0 in range(0, width, FF_TILE):
            cw = min(FF_TILE, width - c0)
            y = _dot(xb, w_ref[:, start + c0:start + c0 + cw])
            o_ref[:, c0:c0 + cw] = y.astype(o_ref.dtype)


def _proj(x, w, segs, dtypes):
    n = _rows(x)
    args = _row_args(x)
    return pl.pallas_call(
        functools.partial(_proj_kernel, segs=segs, nx=len(args)),
        grid=(n // ROW_TILE,),
        in_specs=_row_specs(x) + [pl.BlockSpec(w.shape, lambda i: (0, 0))],
        out_specs=[pl.BlockSpec((ROW_TILE, wd), lambda i: (i, 0)) for _, wd in segs],
        out_shape=[jax.ShapeDtypeStruct((n, wd), dt) for (_, wd), dt in zip(segs, dtypes)],
        compiler_params=_params("arbitrary"),
        name="proj",
    )(*args, w)


def _outproj_ln_kernel(*refs, nh, nx):
    w_ref, g_ref, b_ref, o_ref = refs[nh + nx:]
    y = _dot(_row_load(refs[:nh]), w_ref[...])
    o_ref[...] = _layer_norm(ALPHA * _row_load(refs[nh:nh + nx]) + y, g_ref[...], b_ref[...])


def _outproj_ln(h, x, w, g, b):
    kh, d = w.shape
    n = _rows(x)
    const = lambda i: (0, 0)
    h_args, x_args = _row_args(h), _row_args(x)
    return pl.pallas_call(
        functools.partial(_outproj_ln_kernel, nh=len(h_args), nx=len(x_args)),
        grid=(n // ROW_TILE,),
        in_specs=_row_specs(h) + _row_specs(x) + [pl.BlockSpec((kh, d), const),
                                                  pl.BlockSpec((1, d), const), pl.BlockSpec((1, d), const)],
        out_specs=pl.BlockSpec((ROW_TILE, d), lambda i: (i, 0)),
        out_shape=jax.ShapeDtypeStruct((n, d), F32),
        compiler_params=_params("arbitrary"),
        name="outproj_ln",
    )(*h_args, *x_args, w, g, b)


def _mlp_ln_kernel(x_ref, w1_ref, w2_ref, g_ref, b_ref, *out_and_scratch, split):
    acc_ref = out_and_scratch[-1]
    x = x_ref[...]
    xb = x.astype(BF16)
    dff = w1_ref.shape[1]
    for c0 in range(0, dff, FF_TILE):
        h = _dot(xb, w1_ref[:, c0:c0 + FF_TILE])
        h = jnp.square(jnp.maximum(h, 0.0)).astype(BF16)
        y = _dot(h, w2_ref[c0:c0 + FF_TILE, :])
        if c0 == 0:
            acc_ref[...] = y
        else:
            acc_ref[...] += y
    res = _layer_norm(ALPHA * x + acc_ref[...], g_ref[...], b_ref[...])
    if not split:
        out_and_scratch[0][...] = res
        return
    main_ref, tail_ref = out_and_scratch[:2]
    on_tail = pl.program_id(0) == pl.num_programs(0) - 1

    @pl.when(jnp.logical_not(on_tail))
    def _():
        main_ref[...] = res

    @pl.when(on_tail)
    def _():
        tail_ref[...] = res


def _mlp_ln(x, w1, w2, g, b, split_rows=None):
    n = x.shape[0]
    d, dff = w1.shape
    const = lambda i: (0, 0)
    if split_rows is None:
        out_specs = pl.BlockSpec((ROW_TILE, d), lambda i: (i, 0))
        out_shape = jax.ShapeDtypeStruct((n, d), F32)
    else:
        assert n - split_rows == ROW_TILE
        last = split_rows // ROW_TILE - 1
        out_specs = [pl.BlockSpec((ROW_TILE, d), lambda i: (jnp.minimum(i, last), 0)),
                     pl.BlockSpec((ROW_TILE, d), lambda i: (0, 0))]
        out_shape = [jax.ShapeDtypeStruct((split_rows, d), F32), jax.ShapeDtypeStruct((ROW_TILE, d), F32)]
    return pl.pallas_call(
        functools.partial(_mlp_ln_kernel, split=split_rows is not None),
        grid=(n // ROW_TILE,),
        in_specs=[pl.BlockSpec((ROW_TILE, d), lambda i: (i, 0)),
                  pl.BlockSpec((d, dff), const), pl.BlockSpec((dff, d), const),
                  pl.BlockSpec((1, d), const), pl.BlockSpec((1, d), const)],
        out_specs=out_specs,
        out_shape=out_shape,
        scratch_shapes=[pltpu.VMEM((ROW_TILE, d), F32)],
        compiler_params=_params("arbitrary"),
        name="mlp_ln",
    )(x, w1, w2, g, b)


def _tri_masks(n):
    row = lax.broadcasted_iota(jnp.int32, (n, n), 0)
    col = lax.broadcasted_iota(jnp.int32, (n, n), 1)
    return row >= col, row > col, row == col


def _mlstm_gates(g_raw, gb):
    cap = GATE_CAP * jnp.tanh((g_raw + gb) / GATE_CAP)
    return cap, jax.nn.log_sigmoid(cap)


def _mlstm_chunk(seqs, gb_ref, nw_ref, state_in, state_out, scratch, length, valid):
    c_in, nb_in, m_in = state_in
    c_o, nb_o, m_o = state_out
    q_s, k_s, vx_s, e_s, kwt_s, wi_s, em_s, dc_s = scratch
    nseq = len(seqs)
    groups = nseq * H_A
    causal, _, _ = _tri_masks(length)
    tri_bf = causal.astype(BF16)
    scale = DK_A ** -0.5
    row_id = lax.broadcasted_iota(jnp.int32, (length, LANES), 0)
    lane_id = lax.broadcasted_iota(jnp.int32, (length, LANES), 1)
    head_lanes = (lane_id >= H_A) & (lane_id < 2 * H_A)
    ones = jnp.ones((length, DV_A), BF16)

    for j, (q_ref, k_ref, v_ref, _, g_ref, rows, _) in enumerate(seqs):
        cap, lf = _mlstm_gates(g_ref[rows, :], gb_ref[...])
        i_sh = pltpu.roll(cap, H_A, axis=1)
        if valid < length:
            i_sh = jnp.where(row_id < valid, i_sh, NEG_INF)
            lf = jnp.where(row_id < valid, lf, 0.0)
        i_sh = jnp.where(head_lanes, i_sh, 0.0)
        bcum = jnp.where(head_lanes, _cumsum_rows(tri_bf, lf), 0.0)
        cmat = i_sh - bcum
        cmax = cmat
        span = 1
        while span < length:
            cmax = jnp.maximum(cmax, jnp.where(row_id >= span, pltpu.roll(cmax, span, axis=0), NEG_INF))
            span *= 2
        m_prev = m_in[j]
        m_t = bcum + jnp.maximum(m_prev, cmax)
        a_mat = bcum - m_t
        w_inter = jnp.exp(bcum + m_prev - m_t)
        e_m = jnp.exp(-m_t)
        m_new = m_t[length - 1:length, :]
        b_last = bcum[length - 1:length, :]
        w_k = jnp.exp(b_last - bcum + i_sh - m_new)
        decay = jnp.exp(b_last + m_prev - m_new)
        m_o[j] = m_new
        cmat_t = cmat.T
        for h in range(H_A):
            g = j * H_A + h
            lane = H_A + h
            q_s[g] = q_ref[rows, h * DK_A:(h + 1) * DK_A]
            kh = k_ref[rows, h * DK_A:(h + 1) * DK_A]
            k_s[g] = kh
            vx_s[g, :, 0:DV_A] = v_ref[rows, h * DV_A:(h + 1) * DV_A]
            vx_s[g, :, DV_A:2 * DV_A] = ones
            e_s[g] = jnp.exp(jnp.where(causal, a_mat[:, lane:lane + 1] + cmat_t[lane:lane + 1, :], NEG_INF))
            wi_s[g] = jnp.broadcast_to(w_inter[:, lane:lane + 1], (length, DV_A))
            em_s[g] = jnp.broadcast_to(e_m[:, lane:lane + 1], (length, DV_A))
            kwt_s[g] = (kh.astype(F32) * scale * w_k[:, lane:lane + 1]).T.astype(BF16)
            dc_s[g] = jnp.broadcast_to(decay[:, lane:lane + 1], (1, DV_A))
    qb = q_s[...]
    qk = _bdot_nt(qb, k_s[...])
    c_old = c_in[...].reshape(groups, DK_A, DV_A)
    nb_old = nb_in[...]
    qcx = _bdot(qb, jnp.concatenate([c_old, nb_old], axis=-1).astype(BF16))
    sw = (qk * scale * e_s[...]).astype(BF16)
    vx = vx_s[...]
    svx = _bdot(sw, vx)
    kvx = _bdot(kwt_s[...], vx)
    wi = wi_s[...]
    num = svx[:, :, 0:DV_A] + wi * qcx[:, :, 0:DV_A]
    den = svx[:, :, DV_A:2 * DV_A] + wi * qcx[:, :, DV_A:2 * DV_A]
    hh = num / jnp.maximum(jnp.abs(den), em_s[...])
    dc = dc_s[...]
    c_o[...] = (dc * c_old + kvx[:, :, 0:DV_A]).reshape(c_o.shape)
    nb_o[...] = dc * nb_old + kvx[:, :, DV_A:2 * DV_A]
    hn = _rms(hh) * nw_ref[...]
    for j, (_, _, _, o_ref, _, rows, write_hg) in enumerate(seqs):
        for h in range(H_A):
            cols = slice(h * DV_A, (h + 1) * DV_A)
            gate = _sigmoid(o_ref[rows, cols].astype(F32))
            write_hg(cols, gate * hn[j * H_A + h])


def _n_to_lanes(n_row):
    return jnp.broadcast_to(n_row, (LANES, DK_A)).T


def _n_from_lanes(nb):
    return nb.T[0:1, :]


def _mlstm_scratch(groups, length):
    return [pltpu.VMEM((groups, length, DK_A), BF16),
            pltpu.VMEM((groups, length, DK_A), BF16),
            pltpu.VMEM((groups, length, 2 * DV_A), BF16),
            pltpu.VMEM((groups, length, length), F32),
            pltpu.VMEM((groups, DK_A, length), BF16),
            pltpu.VMEM((groups, length, DV_A), F32),
            pltpu.VMEM((groups, length, DV_A), F32),
            pltpu.VMEM((groups, 1, DV_A), F32)]


def _mlstm_prompt_kernel(*refs, nseq):
    ins = refs[:5 * nseq]
    gb_ref, nw_ref, hg_ref, c_out, n_out, m_out, c_s, n_s, m_s = refs[5 * nseq:5 * nseq + 9]
    scratch = refs[5 * nseq + 9:]
    t = pl.program_id(1)

    @pl.when(t == 0)
    def _():
        c_s[...] = jnp.zeros_like(c_s)
        n_s[...] = jnp.zeros_like(n_s)
        m_s[...] = jnp.zeros_like(m_s)

    def chunk(c, carry):
        rows = pl.ds(pl.multiple_of(c * CHUNK, CHUNK), CHUNK)
        seqs = []
        for j in range(nseq):
            def write_hg(cols, val, j=j):
                hg_ref[j, rows, cols] = val.astype(hg_ref.dtype)
            seqs.append(tuple(ins[i * nseq + j] for i in range(5)) + (rows, write_hg))
        state = (c_s, n_s, m_s)
        _mlstm_chunk(seqs, gb_ref, nw_ref, state, state, scratch, CHUNK, CHUNK)
        return carry

    lax.fori_loop(0, hg_ref.shape[1] // CHUNK, chunk, 0)

    @pl.when(t == pl.num_programs(1) - 1)
    def _():
        c_out[...] = c_s[...].reshape(c_out.shape)
        for j in range(nseq):
            for h in range(H_A):
                n_out[j, h:h + 1, :] = _n_from_lanes(n_s[j * H_A + h])
        m_out[...] = m_s[...]


def _mlstm_prompt(q, k, v, o, g, gb, nw, batch, seq):
    tb = min(MLSTM_ROWS, seq)
    nt = seq // tb
    nseq = MLSTM_SEQS if batch % MLSTM_SEQS == 0 else 1
    groups = nseq * H_A
    const = lambda b, t: (0, 0)

    def rows_of(j):
        return lambda b, t: ((b * nseq + j) * nt + t, 0)

    in_specs, args = [], []
    for arr, width in ((q, QK_A), (k, QK_A), (v, VW_A), (o, VW_A), (g, LANES)):
        for j in range(nseq):
            in_specs.append(pl.BlockSpec((tb, width), rows_of(j)))
            args.append(arr)
    in_specs += [pl.BlockSpec((1, LANES), const), pl.BlockSpec((groups, 1, DV_A), lambda b, t: (0, 0, 0))]
    args += [gb, jnp.tile(nw.reshape(H_A, 1, DV_A), (nseq, 1, 1))]
    hg, c, n, m = pl.pallas_call(
        functools.partial(_mlstm_prompt_kernel, nseq=nseq),
        grid=(batch // nseq, nt),
        in_specs=in_specs,
        out_specs=[pl.BlockSpec((nseq, tb, VW_A), lambda b, t: (b, t, 0)),
                   pl.BlockSpec((nseq, H_A, DK_A, DV_A), lambda b, t: (b, 0, 0, 0)),
                   pl.BlockSpec((nseq, H_A, DK_A), lambda b, t: (b, 0, 0)),
                   pl.BlockSpec((nseq, 1, LANES), lambda b, t: (b, 0, 0))],
        out_shape=[jax.ShapeDtypeStruct((batch, seq, VW_A), BF16),
                   jax.ShapeDtypeStruct((batch, H_A, DK_A, DV_A), F32),
                   jax.ShapeDtypeStruct((batch, H_A, DK_A), F32),
                   jax.ShapeDtypeStruct((batch, 1, LANES), F32)],
        scratch_shapes=[pltpu.VMEM((groups, DK_A, DV_A), F32),
                        pltpu.VMEM((groups, DK_A, LANES), F32),
                        pltpu.VMEM((nseq, 1, LANES), F32)]
        + _mlstm_scratch(groups, CHUNK),
        compiler_params=_params("parallel", "arbitrary"),
        name="mlstm_prompt",
    )(*args)
    return hg.reshape(batch * seq, VW_A), c, n, m


def _mlstm_sample_kernel(q_ref, k_ref, v_ref, o_ref, g_ref, c0_ref, n0_ref, m0_ref, gb_ref, nw_ref, carry_ref,
                         hg_ref, c_out, n_out, m_out, nb_s, *scratch, nseq, length, valid):
    del carry_ref
    seqs = []
    for j in range(nseq):
        rows = slice(j * length, (j + 1) * length)

        def write_hg(cols, val, rows=rows):
            hg_ref[rows, cols] = val.astype(hg_ref.dtype)
        seqs.append((q_ref, k_ref, v_ref, o_ref, g_ref, rows, write_hg))
        for h in range(H_A):
            nb_s[j * H_A + h] = _n_to_lanes(n0_ref[j, h:h + 1, :])
    _mlstm_chunk(seqs, gb_ref, nw_ref, (c0_ref, nb_s, m0_ref), (c_out, nb_s, m_out), scratch, length, valid)
    for j in range(nseq):
        for h in range(H_A):
            n_out[j, h:h + 1, :] = _n_from_lanes(nb_s[j * H_A + h])


def _mlstm_sample(q, k, v, o, g, c_all, layer, c_carry, n0, m0, gb, nw, length, valid):
    batch = n0.shape[0]
    nseq = MLSTM_SAMPLE_SEQS if batch % MLSTM_SAMPLE_SEQS == 0 else 1
    groups = nseq * H_A
    rows = nseq * length
    row = lambda b: (b, 0)
    b3 = lambda b: (b, 0, 0)
    c_spec = pl.BlockSpec((None, nseq, H_A, DK_A, DV_A), lambda b: (layer, b, 0, 0, 0))
    args = [q, k, v, o, g, c_all, n0, m0, gb, jnp.tile(nw.reshape(H_A, 1, DV_A), (nseq, 1, 1))]
    in_specs = [pl.BlockSpec((rows, QK_A), row), pl.BlockSpec((rows, QK_A), row),
                pl.BlockSpec((rows, VW_A), row), pl.BlockSpec((rows, VW_A), row),
                pl.BlockSpec((rows, LANES), row),
                c_spec,
                pl.BlockSpec((nseq, H_A, DK_A), b3),
                pl.BlockSpec((nseq, 1, LANES), b3),
                pl.BlockSpec((1, LANES), lambda b: (0, 0)),
                pl.BlockSpec((groups, 1, DV_A), lambda b: (0, 0, 0))]
    aliases = {len(args): 1}
    args.append(c_carry)
    in_specs.append(pl.BlockSpec(memory_space=pl.ANY))
    return pl.pallas_call(
        functools.partial(_mlstm_sample_kernel, nseq=nseq, length=length, valid=valid),
        grid=(batch // nseq,),
        in_specs=in_specs,
        out_specs=[pl.BlockSpec((rows, VW_A), row),
                   c_spec,
                   pl.BlockSpec((nseq, H_A, DK_A), b3),
                   pl.BlockSpec((nseq, 1, LANES), b3)],
        out_shape=[jax.ShapeDtypeStruct((batch * length, VW_A), BF16),
                   jax.ShapeDtypeStruct(c_all.shape, F32),
                   jax.ShapeDtypeStruct((batch, H_A, DK_A), F32),
                   jax.ShapeDtypeStruct((batch, 1, LANES), F32)],
        scratch_shapes=[pltpu.VMEM((groups, DK_A, LANES), F32)] + _mlstm_scratch(groups, length),
        input_output_aliases=aliases,
        compiler_params=_params("arbitrary"),
        name="mlstm_sample",
    )(*args)


def _gdn_gates(ba, alog, dtb):
    beta = _sigmoid(ba)
    g = -jnp.exp(alog) * jax.nn.softplus(ba + dtb)
    return beta, g


def _bdot(a, b):
    return jnp.einsum('hmk,hkn->hmn', a, b, preferred_element_type=F32)


def _bdot_nt(a, b):
    return jnp.einsum('hmk,hnk->hmn', a, b, preferred_element_type=F32)


def _unit_lower_solve(nmat, x, order):
    span = 1
    while span < order:
        nh = nmat.astype(BF16)
        xh, xl = _split2(x)
        x = x + (_bdot(nh, xh) + _bdot(nh, xl))
        span *= 2
        if span < order:
            nmat = _bdot(nh, nh)
    return x


CONV_HALO = SUBLANES


def _gdn_chunk(seqs, cw_ref, dtb_ref, alog_ref, nw_ref, s_in, s_o, scratch, length, valid):
    q_s, k_s, bv_s, beg_s, n_s, attn_s, kq_s, kdt_s, gl_s = scratch
    nseq = len(seqs)
    groups = nseq * HV_B
    tril, strict, _ = _tri_masks(length)
    tri_bf = tril.astype(BF16)
    scale = DK_B ** -0.5
    rep = HV_B // HK_B
    live = lax.broadcasted_iota(jnp.int32, (length, LANES), 0) < valid

    gates = []
    for j, (conv_rows, _, ba_ref, rows, _) in enumerate(seqs):
        beta_all, g_all = _gdn_gates(ba_ref[rows, :], alog_ref[...], dtb_ref[...])
        if valid < length:
            beta_all = jnp.where(live, beta_all, 0.0)
            g_all = jnp.where(live, g_all, 0.0)
        gc = _cumsum_rows(tri_bf, g_all)
        g_last = gc[length - 1:length, :]
        gates.append((beta_all, gc, gc.T, jnp.exp(gc), jnp.exp(g_last - gc), jnp.exp(g_last)))
        for kh in range(HK_B):
            q_s[j * HK_B + kh] = _l2norm(_conv_silu(conv_rows(kh * DK_B), cw_ref, kh * DK_B, length)) * scale
            k_s[j * HK_B + kh] = _l2norm(_conv_silu(conv_rows(QK_B + kh * DK_B), cw_ref, QK_B + kh * DK_B, length))
    kb = k_s[...].astype(BF16)
    kk = _bdot_nt(kb, kb)
    qk = _bdot_nt(q_s[...].astype(BF16), kb)

    for j, (conv_rows, _, _, _, _) in enumerate(seqs):
        beta_all, gc, gc_t, eg_all, egd_all, gl_all = gates[j]
        for hv in range(HV_B):
            g = j * HV_B + hv
            kh = j * HK_B + hv // rep
            gl = HV_B + hv
            col0 = 2 * QK_B + hv * DV_B
            v = _conv_silu(conv_rows(col0), cw_ref, col0, length)
            k = k_s[kh]
            beta = beta_all[:, hv:hv + 1]
            eg = eg_all[:, gl:gl + 1]
            decay = jnp.exp(jnp.where(tril, gc[:, gl:gl + 1] - gc_t[gl:gl + 1, :], NEG_INF))
            n_s[g] = jnp.where(strict, -(kk[kh] * beta) * decay, 0.0)
            bv_s[g] = v * beta
            beg_s[g] = jnp.broadcast_to(beta * eg, (length, DV_B))
            attn_s[g] = (qk[kh] * decay).astype(BF16)
            kq_s[g, 0:length, :] = k.astype(BF16)
            kq_s[g, length:2 * length, :] = (q_s[kh] * eg).astype(BF16)
            kdt_s[g] = (k * egd_all[:, gl:gl + 1]).T.astype(BF16)
            gl_s[g] = jnp.broadcast_to(gl_all[:, gl:gl + 1], (1, LANES))

    s = s_in[...].reshape(groups, DK_B, DV_B)
    ks = _bdot(kq_s[...], s.astype(BF16))
    v_new = _unit_lower_solve(n_s[...], bv_s[...] - beg_s[...] * ks[:, 0:length, :], valid)
    vb = v_new.astype(BF16)
    o = ks[:, length:2 * length, :] + _bdot(attn_s[...], vb)
    s_o[...] = (gl_s[...] * s + _bdot(kdt_s[...], vb)).reshape(s_o.shape)
    for j, (_, z_ref, _, rows, write_o) in enumerate(seqs):
        for hv in range(HV_B):
            cols = slice(hv * DV_B, (hv + 1) * DV_B)
            zz = z_ref[rows, cols].astype(F32)
            write_o(cols, _rms(o[j * HV_B + hv]) * nw_ref[:, cols] * (zz * _sigmoid(zz)))


def _conv_silu(y, cw_ref, col0, length):
    cw = cw_ref[:, col0:col0 + LANES]
    first = CONV_HALO - (CONV_W - 1)
    acc = y[first:first + length] * cw[0:1]
    for j in range(1, CONV_W):
        acc = acc + y[first + j:first + j + length] * cw[j:j + 1]
    return acc * _sigmoid(acc)


def _gdn_scratch(nseq, length):
    groups = nseq * HV_B
    return [pltpu.VMEM((nseq * HK_B, length, DK_B), F32),
            pltpu.VMEM((nseq * HK_B, length, DK_B), F32),
            pltpu.VMEM((groups, length, DV_B), F32),
            pltpu.VMEM((groups, length, DV_B), F32),
            pltpu.VMEM((groups, length, length), F32),
            pltpu.VMEM((groups, length, length), BF16),
            pltpu.VMEM((groups, 2 * length, DK_B), BF16),
            pltpu.VMEM((groups, DK_B, length), BF16),
            pltpu.VMEM((groups, 1, LANES), F32)]


def _gdn_prompt_kernel(qkv_ref, z_ref, ba_ref, cw_ref, dtb_ref, alog_ref, nw_ref,
                       o_ref, s_out, conv_out, xpad_s, s_s, *scratch):
    t = pl.program_id(1)
    tb = qkv_ref.shape[0]

    @pl.when(t == 0)
    def _():
        s_s[...] = jnp.zeros_like(s_s)
        xpad_s[0:CONV_HALO, :] = jnp.zeros((CONV_HALO, CONV_DIM), F32)

    xpad_s[CONV_HALO:CONV_HALO + tb, :] = qkv_ref[...].astype(F32)

    def chunk(c, carry):
        r0 = pl.multiple_of(c * CHUNK, CHUNK)
        rows = pl.ds(r0, CHUNK)

        def conv_rows(col0):
            return xpad_s[pl.ds(r0, CONV_HALO + CHUNK), col0:col0 + LANES]

        def write_o(cols, val):
            o_ref[rows, cols] = val.astype(o_ref.dtype)

        _gdn_chunk([(conv_rows, z_ref, ba_ref, rows, write_o)], cw_ref, dtb_ref, alog_ref, nw_ref,
                   s_s, s_s, scratch, CHUNK, CHUNK)
        return carry

    lax.fori_loop(0, tb // CHUNK, chunk, 0)

    xpad_s[0:CONV_HALO, :] = xpad_s[tb:tb + CONV_HALO, :]

    @pl.when(t == pl.num_programs(1) - 1)
    def _():
        s_out[0] = s_s[...]
        conv_out[0] = xpad_s[tb + CONV_HALO - (CONV_W - 1):tb + CONV_HALO, :]


def _gdn_prompt(qkv, z, ba, cw, dtb, alog, nw, batch, seq):
    tb = min(GDN_ROWS, seq)
    nt = seq // tb
    row = lambda b, t: (b * nt + t, 0)
    const = lambda b, t: (0, 0)
    return pl.pallas_call(
        _gdn_prompt_kernel,
        grid=(batch, nt),
        in_specs=[pl.BlockSpec((tb, CONV_DIM), row), pl.BlockSpec((tb, VW_B), row),
                  pl.BlockSpec((tb, LANES), row),
                  pl.BlockSpec((CONV_W, CONV_DIM), const),
                  pl.BlockSpec((1, LANES), const), pl.BlockSpec((1, LANES), const),
                  pl.BlockSpec((1, VW_B), const)],
        out_specs=[pl.BlockSpec((tb, VW_B), row),
                   pl.BlockSpec((1, HV_B, DK_B, DV_B), lambda b, t: (b, 0, 0, 0)),
                   pl.BlockSpec((1, CONV_W - 1, CONV_DIM), lambda b, t: (b, 0, 0))],
        out_shape=[jax.ShapeDtypeStruct((batch * seq, VW_B), BF16),
                   jax.ShapeDtypeStruct((batch, HV_B, DK_B, DV_B), F32),
                   jax.ShapeDtypeStruct((batch, CONV_W - 1, CONV_DIM), F32)],
        scratch_shapes=[pltpu.VMEM((tb + CONV_HALO, CONV_DIM), F32),
                        pltpu.VMEM((HV_B, DK_B, DV_B), F32)]
        + _gdn_scratch(1, CHUNK),
        compiler_params=_params("parallel", "arbitrary"),
        name="gdn_prompt",
    )(qkv, z, ba, cw, dtb, alog, nw)


def _gdn_sample_kernel(xp_ref, z_ref, ba_ref, s0_ref, cw_ref, dtb_ref, alog_ref, nw_ref, carry_ref,
                       o_ref, s_out, *scratch, nseq, length, valid):
    del carry_ref
    seqs = []
    for j in range(nseq):
        rows = slice(j * length, (j + 1) * length)

        def conv_rows(col0, j=j):
            return xp_ref[j, :, col0:col0 + LANES]

        def write_o(cols, val, rows=rows):
            o_ref[rows, cols] = val.astype(o_ref.dtype)
        seqs.append((conv_rows, z_ref, ba_ref, rows, write_o))
    _gdn_chunk(seqs, cw_ref, dtb_ref, alog_ref, nw_ref, s0_ref, s_out, scratch, length, valid)


def _gdn_sample(xp, z, ba, s_all, layer, s_carry, cw, dtb, alog, nw, length, valid):
    batch = xp.shape[0]
    nseq = GDN_SAMPLE_SEQS if batch % GDN_SAMPLE_SEQS == 0 else 1
    rows = nseq * length
    row = lambda b: (b, 0)
    const = lambda b: (0, 0)
    s_spec = pl.BlockSpec((None, nseq, HV_B, DK_B, DV_B), lambda b: (layer, b, 0, 0, 0))
    args = [xp, z, ba, s_all, cw, dtb, alog, nw]
    in_specs = [pl.BlockSpec((nseq,) + xp.shape[1:], lambda b: (b, 0, 0)),
                pl.BlockSpec((rows, VW_B), row),
                pl.BlockSpec((rows, LANES), row),
                s_spec,
                pl.BlockSpec((CONV_W, CONV_DIM), const),
                pl.BlockSpec((1, LANES), const), pl.BlockSpec((1, LANES), const),
                pl.BlockSpec((1, VW_B), const)]
    aliases = {len(args): 1}
    args.append(s_carry)
    in_specs.append(pl.BlockSpec(memory_space=pl.ANY))
    return pl.pallas_call(
        functools.partial(_gdn_sample_kernel, nseq=nseq, length=length, valid=valid),
        grid=(batch // nseq,),
        in_specs=in_specs,
        out_specs=[pl.BlockSpec((rows, VW_B), row), s_spec],
        out_shape=[jax.ShapeDtypeStruct((batch * length, VW_B), BF16),
                   jax.ShapeDtypeStruct(s_all.shape, F32)],
        scratch_shapes=_gdn_scratch(nseq, length),
        input_output_aliases=aliases,
        compiler_params=_params("arbitrary"),
        name="gdn_sample",
    )(*args)


def _pad_cols(w, width):
    return jnp.pad(w, ((0, 0), (0, width - w.shape[1])))


def _lane_row(vec, offset):
    return jnp.zeros((1, LANES), F32).at[0, offset:offset + vec.shape[0]].set(vec.astype(F32))


MLSTM_SEGS = ((0, QK_A), (QK_A, QK_A), (2 * QK_A, VW_A), (2 * QK_A + VW_A, VW_A), (2 * QK_A + 2 * VW_A, LANES))
GDN_SEGS = ((0, CONV_DIM), (CONV_DIM, VW_B), (CONV_DIM + VW_B, LANES))


def _mlstm_layer(x, batch, seq, s_batch, s_seq, w_in, gate_b, norm_w, w_out, c_all, layer, c_carry, n0, m0,
                 ln_g, ln_b):
    w = _pad_cols(w_in, MLSTM_SEGS[-1][0] + LANES).astype(BF16)
    q, k, v, o, g = _proj(x, w, MLSTM_SEGS, (BF16, BF16, BF16, BF16, F32))
    gb = _lane_row(gate_b, 0)
    nw = norm_w.astype(F32).reshape(1, VW_A)
    hg_p, c_p, n_p, m_p = _mlstm_prompt(q, k, v, o, g, gb, nw, batch, seq)

    def padded(a):
        a = a[batch * seq:].reshape(s_batch, s_seq, a.shape[1])
        return jnp.pad(a, ((0, 0), (0, SAMPLE_CHUNK - s_seq), (0, 0))).reshape(s_batch * SAMPLE_CHUNK, a.shape[2])

    m0_lanes = jnp.pad(m0.astype(F32), ((0, 0), (H_A, LANES - 2 * H_A))).reshape(s_batch, 1, LANES)
    hg_s, c_s, n_s, m_s = _mlstm_sample(padded(q), padded(k), padded(v), padded(o), padded(g), c_all, layer,
                                        c_carry, n0, m0_lanes, gb, nw, SAMPLE_CHUNK, s_seq)
    hg_s = hg_s.reshape(s_batch, SAMPLE_CHUNK, VW_A)[:, :s_seq].reshape(s_batch * s_seq, VW_A)
    x = _outproj_ln((hg_p, hg_s), x, w_out.astype(BF16), ln_g.reshape(1, -1), ln_b.reshape(1, -1))
    return x, (c_p, n_p, m_p[:, 0, H_A:2 * H_A]), (c_s, n_s, m_s[:, 0, H_A:2 * H_A])


def _gdn_layer(x, batch, seq, s_batch, s_seq, w_in, conv_w, dt_bias, a_log, norm_w, w_out,
               s_all, layer, s_carry, conv0, ln_g, ln_b):
    w = _pad_cols(w_in, GDN_SEGS[-1][0] + LANES).astype(BF16)
    qkv, z, ba = _proj(x, w, GDN_SEGS, (BF16, BF16, F32))
    cw = conv_w.astype(F32)
    dtb = _lane_row(dt_bias, HV_B)
    alog = _lane_row(a_log, HV_B)
    nw = norm_w.astype(F32).reshape(1, VW_B)
    o_p, s_p, conv_p = _gdn_prompt(qkv, z, ba, cw, dtb, alog, nw, batch, seq)

    def tail(a):
        return a[batch * seq:].reshape(s_batch, s_seq, a.shape[1])

    def padded(a):
        return jnp.pad(tail(a), ((0, 0), (0, SAMPLE_CHUNK - s_seq), (0, 0))).reshape(s_batch * SAMPLE_CHUNK, a.shape[1])

    xp = jnp.concatenate([conv0.astype(F32), tail(qkv).astype(F32)], axis=1)
    conv_s = xp[:, s_seq:]
    front = CONV_HALO - (CONV_W - 1)
    xp = jnp.pad(xp, ((0, 0), (front, SAMPLE_CHUNK - s_seq), (0, 0)))
    o_s, s_s = _gdn_sample(xp, padded(z), padded(ba), s_all, layer, s_carry, cw, dtb, alog, nw,
                           SAMPLE_CHUNK, s_seq)
    o_s = o_s.reshape(s_batch, SAMPLE_CHUNK, VW_B)[:, :s_seq].reshape(s_batch * s_seq, VW_B)
    x = _outproj_ln((o_p, o_s), x, w_out.astype(BF16), ln_g.reshape(1, -1), ln_b.reshape(1, -1))
    return x, (s_p, conv_p), (s_s, conv_s)


def kernel(x_prompt, x_sample, state_mlstm_C, state_mlstm_n, state_mlstm_m, state_gdn_S, state_gdn_conv,
           a_w_in, a_gate_b, a_norm_w, a_w_out, b_w_in, b_conv_w, b_dt_bias, b_a_log, b_norm_w, b_w_out,
           mlp_w1, mlp_w2, ln1_g, ln1_b, ln2_g, ln2_b):
    batch, seq, d = x_prompt.shape
    s_batch, s_seq, _ = x_sample.shape
    n_prompt = batch * seq
    n_sample = s_batch * s_seq
    assert n_prompt % ROW_TILE == 0 and n_sample == ROW_TILE and seq % CHUNK == 0 and s_seq <= SAMPLE_CHUNK
    x = (x_prompt.reshape(n_prompt, d), x_sample.reshape(n_sample, d))
    p_a, s_a, p_b, s_b = [], [], [], []
    c_carry = jnp.zeros(state_mlstm_C.shape, F32)
    s_carry = jnp.zeros(state_gdn_S.shape, F32)
    for layer in range(DEPTH):
        j = layer // 2
        if layer % 2 == 0:
            x, p_state, s_state = _mlstm_layer(
                x, batch, seq, s_batch, s_seq, a_w_in[j], a_gate_b[j], a_norm_w[j], a_w_out[j],
                state_mlstm_C, j, c_carry, state_mlstm_n[j], state_mlstm_m[j], ln1_g[layer], ln1_b[layer])
            c_carry = s_state[0]
            p_a.append(p_state)
            s_a.append(s_state)
        else:
            x, p_state, s_state = _gdn_layer(
                x, batch, seq, s_batch, s_seq, b_w_in[j], b_conv_w[j], b_dt_bias[j], b_a_log[j],
                b_norm_w[j], b_w_out[j], state_gdn_S, j, s_carry, state_gdn_conv[j], ln1_g[layer], ln1_b[layer])
            s_carry = s_state[0]
            p_b.append(p_state)
            s_b.append(s_state)
        x = _mlp_ln(x, mlp_w1[layer].astype(BF16), mlp_w2[layer].astype(BF16),
                    ln2_g[layer].reshape(1, -1), ln2_b[layer].reshape(1, -1),
                    split_rows=n_prompt if layer == DEPTH - 1 else None)
    y_prompt = x[0].reshape(batch, seq, d)
    y_sample = x[1].reshape(s_batch, s_seq, d)

    def stack(states, i):
        return jnp.stack([s[i] for s in states])

    return (y_prompt, y_sample,
            stack(p_a, 0), stack(p_a, 1), stack(p_a, 2), stack(p_b, 0), stack(p_b, 1),
            c_carry, stack(s_a, 1), stack(s_a, 2), s_carry, stack(s_b, 1))
```

```python
import functools

import jax
import jax.numpy as jnp
from jax import lax
from jax.experimental import pallas as pl
from jax.experimental.pallas import tpu as pltpu

F32 = jnp.float32
BF16 = jnp.bfloat16

D_MODEL = 1024
DEPTH = 4
H_A = 8
DV_A = 128
DK_A = 64
QK_A = H_A * DK_A
VW_A = H_A * DV_A
GATE_CAP = 15.0
HK_B = 8
HV_B = 16
DK_B = 128
DV_B = 128
QK_B = HK_B * DK_B
VW_B = HV_B * DV_B
CONV_W = 4
CONV_DIM = 2 * QK_B + VW_B
CHUNK = 64
D_FF = 4 * D_MODEL
ALPHA = (2.0 * DEPTH) ** 0.25
LN_EPS = 1e-5
RMS_EPS = 1e-6

LANES = 128
SUBLANES = 8
VMEM_LIMIT = 56 * 1024 * 1024
ROW_TILE = 512
FF_TILE = 1024
MLSTM_ROWS = 512
MLSTM_SEQS = 1
MLSTM_SAMPLE_SEQS = 4
GDN_SAMPLE_SEQS = 2
SAMPLE_CHUNK = 16
GDN_ROWS = 256

NEG_INF = float("-inf")


def _params(*sem):
    return pltpu.CompilerParams(dimension_semantics=sem, vmem_limit_bytes=VMEM_LIMIT)


def _dot(a, b):
    return jnp.dot(a, b, preferred_element_type=F32)


def _dot_nt(a, b):
    return lax.dot_general(a, b, (((1,), (1,)), ((), ())), preferred_element_type=F32)


def _dot_tn(a, b):
    return lax.dot_general(a, b, (((0,), (0,)), ((), ())), preferred_element_type=F32)


def _split2(x):
    hi = x.astype(BF16)
    lo = (x - hi.astype(F32)).astype(BF16)
    return hi, lo


def _dot_split(a, b):
    ah, al = _split2(a)
    bh, bl = _split2(b)
    return _dot(ah, bh) + (_dot(ah, bl) + _dot(al, bh))


def _cumsum_rows(tri_bf, x):
    x1 = x.astype(BF16)
    r1 = x - x1.astype(F32)
    x2 = r1.astype(BF16)
    x3 = (r1 - x2.astype(F32)).astype(BF16)
    return _dot(tri_bf, x1) + (_dot(tri_bf, x2) + _dot(tri_bf, x3))


def _layer_norm(r, g, b):
    mu = jnp.mean(r, axis=-1, keepdims=True)
    d = r - mu
    var = jnp.mean(d * d, axis=-1, keepdims=True)
    return d * lax.rsqrt(var + LN_EPS) * g + b


def _rms(h):
    return h * lax.rsqrt(jnp.mean(h * h, axis=-1, keepdims=True) + RMS_EPS)


def _l2norm(x):
    return x * lax.rsqrt(jnp.sum(x * x, axis=-1, keepdims=True) + RMS_EPS)


def _sigmoid(x):
    return 0.5 * jnp.tanh(0.5 * x) + 0.5


def _rows(x):
    return x[0].shape[0] + x[1].shape[0] if isinstance(x, tuple) else x.shape[0]


def _row_args(x):
    return list(x) if isinstance(x, tuple) else [x]


def _row_specs(x):
    if isinstance(x, tuple):
        main, tail = x
        last = main.shape[0] // ROW_TILE - 1
        return [pl.BlockSpec((ROW_TILE, main.shape[1]), lambda i: (jnp.minimum(i, last), 0)),
                pl.BlockSpec((ROW_TILE, tail.shape[1]), lambda i: (0, 0))]
    return [pl.BlockSpec((ROW_TILE, x.shape[1]), lambda i: (i, 0))]


def _row_load(refs):
    if len(refs) == 1:
        return refs[0][...]
    on_tail = pl.program_id(0) == pl.num_programs(0) - 1
    return jnp.where(on_tail, refs[1][...], refs[0][...])


def _proj_kernel(*refs, segs, nx):
    xb = _row_load(refs[:nx]).astype(BF16)
    w_ref = refs[nx]
    out_refs = refs[nx + 1:]
    for (start, width), o_ref in zip(segs, out_refs):
        for c0 in range(0, width, FF_TILE):
            cw = min(FF_TILE, width - c0)
            y = _dot(xb, w_ref[:, start + c0:start + c0 + cw])
            o_ref[:, c0:c0 + cw] = y.astype(o_ref.dtype)


def _proj(x, w, segs, dtypes):
    n = _rows(x)
    args = _row_args(x)
    return pl.pallas_call(
        functools.partial(_proj_kernel, segs=segs, nx=len(args)),
        grid=(n // ROW_TILE,),
        in_specs=_row_specs(x) + [pl.BlockSpec(w.shape, lambda i: (0, 0))],
        out_specs=[pl.BlockSpec((ROW_TILE, wd), lambda i: (i, 0)) for _, wd in segs],
        out_shape=[jax.ShapeDtypeStruct((n, wd), dt) for (_, wd), dt in zip(segs, dtypes)],
        compiler_params=_params("arbitrary"),
        name="proj",
    )(*args, w)


def _outproj_ln_kernel(*refs, nh, nx):
    w_ref, g_ref, b_ref, o_ref = refs[nh + nx:]
    y = _dot(_row_load(refs[:nh]), w_ref[...])
    o_ref[...] = _layer_norm(ALPHA * _row_load(refs[nh:nh + nx]) + y, g_ref[...], b_ref[...])


def _outproj_ln(h, x, w, g, b):
    kh, d = w.shape
    n = _rows(x)
    const = lambda i: (0, 0)
    h_args, x_args = _row_args(h), _row_args(x)
    return pl.pallas_call(
        functools.partial(_outproj_ln_kernel, nh=len(h_args), nx=len(x_args)),
        grid=(n // ROW_TILE,),
        in_specs=_row_specs(h) + _row_specs(x) + [pl.BlockSpec((kh, d), const),
                                                  pl.BlockSpec((1, d), const), pl.BlockSpec((1, d), const)],
        out_specs=pl.BlockSpec((ROW_TILE, d), lambda i: (i, 0)),
        out_shape=jax.ShapeDtypeStruct((n, d), F32),
        compiler_params=_params("arbitrary"),
        name="outproj_ln",
    )(*h_args, *x_args, w, g, b)


def _mlp_ln_kernel(x_ref, w1_ref, w2_ref, g_ref, b_ref, *out_and_scratch, split):
    acc_ref = out_and_scratch[-1]
    x = x_ref[...]
    xb = x.astype(BF16)
    dff = w1_ref.shape[1]
    for c0 in range(0, dff, FF_TILE):
        h = _dot(xb, w1_ref[:, c0:c0 + FF_TILE])
        h = jnp.square(jnp.maximum(h, 0.0)).astype(BF16)
        y = _dot(h, w2_ref[c0:c0 + FF_TILE, :])
        if c0 == 0:
            acc_ref[...] = y
        else:
            acc_ref[...] += y
    res = _layer_norm(ALPHA * x + acc_ref[...], g_ref[...], b_ref[...])
    if not split:
        out_and_scratch[0][...] = res
        return
    main_ref, tail_ref = out_and_scratch[:2]
    on_tail = pl.program_id(0) == pl.num_programs(0) - 1

    @pl.when(jnp.logical_not(on_tail))
    def _():
        main_ref[...] = res

    @pl.when(on_tail)
    def _():
        tail_ref[...] = res


def _mlp_ln(x, w1, w2, g, b, split_rows=None):
    n = x.shape[0]
    d, dff = w1.shape
    const = lambda i: (0, 0)
    if split_rows is None:
        out_specs = pl.BlockSpec((ROW_TILE, d), lambda i: (i, 0))
        out_shape = jax.ShapeDtypeStruct((n, d), F32)
    else:
        assert n - split_rows == ROW_TILE
        last = split_rows // ROW_TILE - 1
        out_specs = [pl.BlockSpec((ROW_TILE, d), lambda i: (jnp.minimum(i, last), 0)),
                     pl.BlockSpec((ROW_TILE, d), lambda i: (0, 0))]
        out_shape = [jax.ShapeDtypeStruct((split_rows, d), F32), jax.ShapeDtypeStruct((ROW_TILE, d), F32)]
    return pl.pallas_call(
        functools.partial(_mlp_ln_kernel, split=split_rows is not None),
        grid=(n // ROW_TILE,),
        in_specs=[pl.BlockSpec((ROW_TILE, d), lambda i: (i, 0)),
                  pl.BlockSpec((d, dff), const), pl.BlockSpec((dff, d), const),
                  pl.BlockSpec((1, d), const), pl.BlockSpec((1, d), const)],
        out_specs=out_specs,
        out_shape=out_shape,
        scratch_shapes=[pltpu.VMEM((ROW_TILE, d), F32)],
        compiler_params=_params("arbitrary"),
        name="mlp_ln",
    )(x, w1, w2, g, b)


def _tri_masks(n):
    row = lax.broadcasted_iota(jnp.int32, (n, n), 0)
    col = lax.broadcasted_iota(jnp.int32, (n, n), 1)
    return row >= col, row > col, row == col


def _mlstm_gates(g_raw, gb):
    cap = GATE_CAP * jnp.tanh((g_raw + gb) / GATE_CAP)
    return cap, jax.nn.log_sigmoid(cap)


def _mlstm_chunk(seqs, gb_ref, nw_ref, state_in, state_out, scratch, length, valid):
    c_in, nb_in, m_in = state_in
    c_o, nb_o, m_o = state_out
    q_s, k_s, vx_s, e_s, kwt_s, wi_s, em_s, dc_s = scratch
    nseq = len(seqs)
    groups = nseq * H_A
    causal, _, _ = _tri_masks(length)
    tri_bf = causal.astype(BF16)
    scale = DK_A ** -0.5
    row_id = lax.broadcasted_iota(jnp.int32, (length, LANES), 0)
    lane_id = lax.broadcasted_iota(jnp.int32, (length, LANES), 1)
    head_lanes = (lane_id >= H_A) & (lane_id < 2 * H_A)
    ones = jnp.ones((length, DV_A), BF16)

    for j, (q_ref, k_ref, v_ref, _, g_ref, rows, _) in enumerate(seqs):
        cap, lf = _mlstm_gates(g_ref[rows, :], gb_ref[...])
        i_sh = pltpu.roll(cap, H_A, axis=1)
        if valid < length:
            i_sh = jnp.where(row_id < valid, i_sh, NEG_INF)
            lf = jnp.where(row_id < valid, lf, 0.0)
        i_sh = jnp.where(head_lanes, i_sh, 0.0)
        bcum = jnp.where(head_lanes, _cumsum_rows(tri_bf, lf), 0.0)
        cmat = i_sh - bcum
        cmax = cmat
        span = 1
        while span < length:
            cmax = jnp.maximum(cmax, jnp.where(row_id >= span, pltpu.roll(cmax, span, axis=0), NEG_INF))
            span *= 2
        m_prev = m_in[j]
        m_t = bcum + jnp.maximum(m_prev, cmax)
        a_mat = bcum - m_t
        w_inter = jnp.exp(bcum + m_prev - m_t)
        e_m = jnp.exp(-m_t)
        m_new = m_t[length - 1:length, :]
        b_last = bcum[length - 1:length, :]
        w_k = jnp.exp(b_last - bcum + i_sh - m_new)
        decay = jnp.exp(b_last + m_prev - m_new)
        m_o[j] = m_new
        cmat_t = cmat.T
        for h in range(H_A):
            g = j * H_A + h
            lane = H_A + h
            q_s[g] = q_ref[rows, h * DK_A:(h + 1) * DK_A]
            kh = k_ref[rows, h * DK_A:(h + 1) * DK_A]
            k_s[g] = kh
            vx_s[g, :, 0:DV_A] = v_ref[rows, h * DV_A:(h + 1) * DV_A]
            vx_s[g, :, DV_A:2 * DV_A] = ones
            e_s[g] = jnp.exp(jnp.where(causal, a_mat[:, lane:lane + 1] + cmat_t[lane:lane + 1, :], NEG_INF))
            wi_s[g] = jnp.broadcast_to(w_inter[:, lane:lane + 1], (length, DV_A))
            em_s[g] = jnp.broadcast_to(e_m[:, lane:lane + 1], (length, DV_A))
            kwt_s[g] = (kh.astype(F32) * scale * w_k[:, lane:lane + 1]).T.astype(BF16)
            dc_s[g] = jnp.broadcast_to(decay[:, lane:lane + 1], (1, DV_A))
    qb = q_s[...]
    qk = _bdot_nt(qb, k_s[...])
    c_old = c_in[...].reshape(groups, DK_A, DV_A)
    nb_old = nb_in[...]
    qcx = _bdot(qb, jnp.concatenate([c_old, nb_old], axis=-1).astype(BF16))
    sw = (qk * scale * e_s[...]).astype(BF16)
    vx = vx_s[...]
    svx = _bdot(sw, vx)
    kvx = _bdot(kwt_s[...], vx)
    wi = wi_s[...]
    num = svx[:, :, 0:DV_A] + wi * qcx[:, :, 0:DV_A]
    den = svx[:, :, DV_A:2 * DV_A] + wi * qcx[:, :, DV_A:2 * DV_A]
    hh = num / jnp.maximum(jnp.abs(den), em_s[...])
    dc = dc_s[...]
    c_o[...] = (dc * c_old + kvx[:, :, 0:DV_A]).reshape(c_o.shape)
    nb_o[...] = dc * nb_old + kvx[:, :, DV_A:2 * DV_A]
    hn = _rms(hh) * nw_ref[...]
    for j, (_, _, _, o_ref, _, rows, write_hg) in enumerate(seqs):
        for h in range(H_A):
            cols = slice(h * DV_A, (h + 1) * DV_A)
            gate = _sigmoid(o_ref[rows, cols].astype(F32))
            write_hg(cols, gate * hn[j * H_A + h])


def _n_to_lanes(n_row):
    return jnp.broadcast_to(n_row, (LANES, DK_A)).T


def _n_from_lanes(nb):
    return nb.T[0:1, :]


def _mlstm_scratch(groups, length):
    return [pltpu.VMEM((groups, length, DK_A), BF16),
            pltpu.VMEM((groups, length, DK_A), BF16),
            pltpu.VMEM((groups, length, 2 * DV_A), BF16),
            pltpu.VMEM((groups, length, length), F32),
            pltpu.VMEM((groups, DK_A, length), BF16),
            pltpu.VMEM((groups, length, DV_A), F32),
            pltpu.VMEM((groups, length, DV_A), F32),
            pltpu.VMEM((groups, 1, DV_A), F32)]


def _mlstm_prompt_kernel(*refs, nseq):
    ins = refs[:5 * nseq]
    gb_ref, nw_ref, hg_ref, c_out, n_out, m_out, c_s, n_s, m_s = refs[5 * nseq:5 * nseq + 9]
    scratch = refs[5 * nseq + 9:]
    t = pl.program_id(1)

    @pl.when(t == 0)
    def _():
        c_s[...] = jnp.zeros_like(c_s)
        n_s[...] = jnp.zeros_like(n_s)
        m_s[...] = jnp.zeros_like(m_s)

    def chunk(c, carry):
        rows = pl.ds(pl.multiple_of(c * CHUNK, CHUNK), CHUNK)
        seqs = []
        for j in range(nseq):
            def write_hg(cols, val, j=j):
                hg_ref[j, rows, cols] = val.astype(hg_ref.dtype)
            seqs.append(tuple(ins[i * nseq + j] for i in range(5)) + (rows, write_hg))
        state = (c_s, n_s, m_s)
        _mlstm_chunk(seqs, gb_ref, nw_ref, state, state, scratch, CHUNK, CHUNK)
        return carry

    lax.fori_loop(0, hg_ref.shape[1] // CHUNK, chunk, 0)

    @pl.when(t == pl.num_programs(1) - 1)
    def _():
        c_out[...] = c_s[...].reshape(c_out.shape)
        for j in range(nseq):
            for h in range(H_A):
                n_out[j, h:h + 1, :] = _n_from_lanes(n_s[j * H_A + h])
        m_out[...] = m_s[...]


def _mlstm_prompt(q, k, v, o, g, gb, nw, batch, seq):
    tb = min(MLSTM_ROWS, seq)
    nt = seq // tb
    nseq = MLSTM_SEQS if batch % MLSTM_SEQS == 0 else 1
    groups = nseq * H_A
    const = lambda b, t: (0, 0)

    def rows_of(j):
        return lambda b, t: ((b * nseq + j) * nt + t, 0)

    in_specs, args = [], []
    for arr, width in ((q, QK_A), (k, QK_A), (v, VW_A), (o, VW_A), (g, LANES)):
        for j in range(nseq):
            in_specs.append(pl.BlockSpec((tb, width), rows_of(j)))
            args.append(arr)
    in_specs += [pl.BlockSpec((1, LANES), const), pl.BlockSpec((groups, 1, DV_A), lambda b, t: (0, 0, 0))]
    args += [gb, jnp.tile(nw.reshape(H_A, 1, DV_A), (nseq, 1, 1))]
    hg, c, n, m = pl.pallas_call(
        functools.partial(_mlstm_prompt_kernel, nseq=nseq),
        grid=(batch // nseq, nt),
        in_specs=in_specs,
        out_specs=[pl.BlockSpec((nseq, tb, VW_A), lambda b, t: (b, t, 0)),
                   pl.BlockSpec((nseq, H_A, DK_A, DV_A), lambda b, t: (b, 0, 0, 0)),
                   pl.BlockSpec((nseq, H_A, DK_A), lambda b, t: (b, 0, 0)),
                   pl.BlockSpec((nseq, 1, LANES), lambda b, t: (b, 0, 0))],
        out_shape=[jax.ShapeDtypeStruct((batch, seq, VW_A), BF16),
                   jax.ShapeDtypeStruct((batch, H_A, DK_A, DV_A), F32),
                   jax.ShapeDtypeStruct((batch, H_A, DK_A), F32),
                   jax.ShapeDtypeStruct((batch, 1, LANES), F32)],
        scratch_shapes=[pltpu.VMEM((groups, DK_A, DV_A), F32),
                        pltpu.VMEM((groups, DK_A, LANES), F32),
                        pltpu.VMEM((nseq, 1, LANES), F32)]
        + _mlstm_scratch(groups, CHUNK),
        compiler_params=_params("parallel", "arbitrary"),
        name="mlstm_prompt",
    )(*args)
    return hg.reshape(batch * seq, VW_A), c, n, m


def _mlstm_sample_kernel(q_ref, k_ref, v_ref, o_ref, g_ref, c0_ref, n0_ref, m0_ref, gb_ref, nw_ref, carry_ref,
                         hg_ref, c_out, n_out, m_out, nb_s, *scratch, nseq, length, valid):
    del carry_ref
    seqs = []
    for j in range(nseq):
        rows = slice(j * length, (j + 1) * length)

        def write_hg(cols, val, rows=rows):
            hg_ref[rows, cols] = val.astype(hg_ref.dtype)
        seqs.append((q_ref, k_ref, v_ref, o_ref, g_ref, rows, write_hg))
        for h in range(H_A):
            nb_s[j * H_A + h] = _n_to_lanes(n0_ref[j, h:h + 1, :])
    _mlstm_chunk(seqs, gb_ref, nw_ref, (c0_ref, nb_s, m0_ref), (c_out, nb_s, m_out), scratch, length, valid)
    for j in range(nseq):
        for h in range(H_A):
            n_out[j, h:h + 1, :] = _n_from_lanes(nb_s[j * H_A + h])


def _mlstm_sample(q, k, v, o, g, c_all, layer, c_carry, n0, m0, gb, nw, length, valid):
    batch = n0.shape[0]
    nseq = MLSTM_SAMPLE_SEQS if batch % MLSTM_SAMPLE_SEQS == 0 else 1
    groups = nseq * H_A
    rows = nseq * length
    row = lambda b: (b, 0)
    b3 = lambda b: (b, 0, 0)
    c_spec = pl.BlockSpec((None, nseq, H_A, DK_A, DV_A), lambda b: (layer, b, 0, 0, 0))
    args = [q, k, v, o, g, c_all, n0, m0, gb, jnp.tile(nw.reshape(H_A, 1, DV_A), (nseq, 1, 1))]
    in_specs = [pl.BlockSpec((rows, QK_A), row), pl.BlockSpec((rows, QK_A), row),
                pl.BlockSpec((rows, VW_A), row), pl.BlockSpec((rows, VW_A), row),
                pl.BlockSpec((rows, LANES), row),
                c_spec,
                pl.BlockSpec((nseq, H_A, DK_A), b3),
                pl.BlockSpec((nseq, 1, LANES), b3),
                pl.BlockSpec((1, LANES), lambda b: (0, 0)),
                pl.BlockSpec((groups, 1, DV_A), lambda b: (0, 0, 0))]
    aliases = {len(args): 1}
    args.append(c_carry)
    in_specs.append(pl.BlockSpec(memory_space=pl.ANY))
    return pl.pallas_call(
        functools.partial(_mlstm_sample_kernel, nseq=nseq, length=length, valid=valid),
        grid=(batch // nseq,),
        in_specs=in_specs,
        out_specs=[pl.BlockSpec((rows, VW_A), row),
                   c_spec,
                   pl.BlockSpec((nseq, H_A, DK_A), b3),
                   pl.BlockSpec((nseq, 1, LANES), b3)],
        out_shape=[jax.ShapeDtypeStruct((batch * length, VW_A), BF16),
                   jax.ShapeDtypeStruct(c_all.shape, F32),
                   jax.ShapeDtypeStruct((batch, H_A, DK_A), F32),
                   jax.ShapeDtypeStruct((batch, 1, LANES), F32)],
        scratch_shapes=[pltpu.VMEM((groups, DK_A, LANES), F32)] + _mlstm_scratch(groups, length),
        input_output_aliases=aliases,
        compiler_params=_params("arbitrary"),
        name="mlstm_sample",
    )(*args)


def _gdn_gates(ba, alog, dtb):
    beta = _sigmoid(ba)
    g = -jnp.exp(alog) * jax.nn.softplus(ba + dtb)
    return beta, g


def _bdot(a, b):
    return jnp.einsum('hmk,hkn->hmn', a, b, preferred_element_type=F32)


def _bdot_nt(a, b):
    return jnp.einsum('hmk,hnk->hmn', a, b, preferred_element_type=F32)


def _unit_lower_solve(nmat, x, order):
    span = 1
    while span < order:
        nh = nmat.astype(BF16)
        xh, xl = _split2(x)
        x = x + (_bdot(nh, xh) + _bdot(nh, xl))
        span *= 2
        if span < order:
            nmat = _bdot(nh, nh)
    return x


CONV_HALO = SUBLANES


def _gdn_chunk(seqs, cw_ref, dtb_ref, alog_ref, nw_ref, s_in, s_o, scratch, length, valid):
    q_s, k_s, bv_s, beg_s, n_s, attn_s, kq_s, kdt_s, gl_s = scratch
    nseq = len(seqs)
    groups = nseq * HV_B
    tril, strict, _ = _tri_masks(length)
    tri_bf = tril.astype(BF16)
    scale = DK_B ** -0.5
    rep = HV_B // HK_B
    live = lax.broadcasted_iota(jnp.int32, (length, LANES), 0) < valid

    gates = []
    for j, (conv_rows, _, ba_ref, rows, _) in enumerate(seqs):
        beta_all, g_all = _gdn_gates(ba_ref[rows, :], alog_ref[...], dtb_ref[...])
        if valid < length:
            beta_all = jnp.where(live, beta_all, 0.0)
            g_all = jnp.where(live, g_all, 0.0)
        gc = _cumsum_rows(tri_bf, g_all)
        g_last = gc[length - 1:length, :]
        gates.append((beta_all, gc, gc.T, jnp.exp(gc), jnp.exp(g_last - gc), jnp.exp(g_last)))
        for kh in range(HK_B):
            q_s[j * HK_B + kh] = _l2norm(_conv_silu(conv_rows, cw_ref, kh * DK_B)) * scale
            k_s[j * HK_B + kh] = _l2norm(_conv_silu(conv_rows, cw_ref, QK_B + kh * DK_B))
    kb = k_s[...].astype(BF16)
    kk = _bdot_nt(kb, kb)
    qk = _bdot_nt(q_s[...].astype(BF16), kb)

    for j, (conv_rows, _, _, _, _) in enumerate(seqs):
        beta_all, gc, gc_t, eg_all, egd_all, gl_all = gates[j]
        for hv in range(HV_B):
            g = j * HV_B + hv
            kh = j * HK_B + hv // rep
            gl = HV_B + hv
            col0 = 2 * QK_B + hv * DV_B
            v = _conv_silu(conv_rows, cw_ref, col0)
            k = k_s[kh]
            beta = beta_all[:, hv:hv + 1]
            eg = eg_all[:, gl:gl + 1]
            decay = jnp.exp(jnp.where(tril, gc[:, gl:gl + 1] - gc_t[gl:gl + 1, :], NEG_INF))
            n_s[g] = jnp.where(strict, -(kk[kh] * beta) * decay, 0.0)
            bv_s[g] = v * beta
            beg_s[g] = jnp.broadcast_to(beta * eg, (length, DV_B))
            attn_s[g] = (qk[kh] * decay).astype(BF16)
            kq_s[g, 0:length, :] = k.astype(BF16)
            kq_s[g, length:2 * length, :] = (q_s[kh] * eg).astype(BF16)
            kdt_s[g] = (k * egd_all[:, gl:gl + 1]).T.astype(BF16)
            gl_s[g] = jnp.broadcast_to(gl_all[:, gl:gl + 1], (1, LANES))

    s = s_in[...].reshape(groups, DK_B, DV_B)
    ks = _bdot(kq_s[...], s.astype(BF16))
    v_new = _unit_lower_solve(n_s[...], bv_s[...] - beg_s[...] * ks[:, 0:length, :], valid)
    vb = v_new.astype(BF16)
    o = ks[:, length:2 * length, :] + _bdot(attn_s[...], vb)
    s_o[...] = (gl_s[...] * s + _bdot(kdt_s[...], vb)).reshape(s_o.shape)
    for j, (_, z_ref, _, rows, write_o) in enumerate(seqs):
        for hv in range(HV_B):
            cols = slice(hv * DV_B, (hv + 1) * DV_B)
            zz = z_ref[rows, cols].astype(F32)
            write_o(cols, _rms(o[j * HV_B + hv]) * nw_ref[:, cols] * (zz * _sigmoid(zz)))


def _conv_silu(conv_rows, cw_ref, col0):
    cw = cw_ref[:, col0:col0 + LANES]
    y = conv_rows(col0)
    length = y.shape[0] - CONV_HALO
    first = CONV_HALO - (CONV_W - 1)
    acc = y[first:first + length] * cw[0:1]
    for j in range(1, CONV_W):
        acc = acc + y[first + j:first + j + length] * cw[j:j + 1]
    return acc * _sigmoid(acc)


def _gdn_scratch(nseq, length):
    groups = nseq * HV_B
    return [pltpu.VMEM((nseq * HK_B, length, DK_B), F32),
            pltpu.VMEM((nseq * HK_B, length, DK_B), F32),
            pltpu.VMEM((groups, length, DV_B), F32),
            pltpu.VMEM((groups, length, DV_B), F32),
            pltpu.VMEM((groups, length, length), F32),
            pltpu.VMEM((groups, length, length), BF16),
            pltpu.VMEM((groups, 2 * length, DK_B), BF16),
            pltpu.VMEM((groups, DK_B, length), BF16),
            pltpu.VMEM((groups, 1, LANES), F32)]


def _gdn_prompt_kernel(qkv_ref, z_ref, ba_ref, cw_ref, dtb_ref, alog_ref, nw_ref,
                       o_ref, s_out, conv_out, xpad_s, s_s, *scratch):
    t = pl.program_id(1)
    tb = qkv_ref.shape[0]

    @pl.when(t == 0)
    def _():
        s_s[...] = jnp.zeros_like(s_s)
        xpad_s[0:CONV_HALO, :] = jnp.zeros((CONV_HALO, CONV_DIM), F32)

    xpad_s[CONV_HALO:CONV_HALO + tb, :] = qkv_ref[...].astype(F32)

    for c in range(tb // CHUNK):
        rows = slice(c * CHUNK, (c + 1) * CHUNK)

        def conv_rows(col0, c=c):
            return xpad_s[c * CHUNK:(c + 1) * CHUNK + CONV_HALO, col0:col0 + LANES]

        def write_o(cols, val, rows=rows):
            o_ref[rows, cols] = val.astype(o_ref.dtype)

        _gdn_chunk([(conv_rows, z_ref, ba_ref, rows, write_o)], cw_ref, dtb_ref, alog_ref, nw_ref,
                   s_s, s_s, scratch, CHUNK, CHUNK)

    xpad_s[0:CONV_HALO, :] = xpad_s[tb:tb + CONV_HALO, :]

    @pl.when(t == pl.num_programs(1) - 1)
    def _():
        s_out[0] = s_s[...]
        conv_out[0] = xpad_s[tb + CONV_HALO - (CONV_W - 1):tb + CONV_HALO, :]


def _gdn_prompt(qkv, z, ba, cw, dtb, alog, nw, batch, seq):
    tb = min(GDN_ROWS, seq)
    nt = seq // tb
    row = lambda b, t: (b * nt + t, 0)
    const = lambda b, t: (0, 0)
    return pl.pallas_call(
        _gdn_prompt_kernel,
        grid=(batch, nt),
        in_specs=[pl.BlockSpec((tb, CONV_DIM), row), pl.BlockSpec((tb, VW_B), row),
                  pl.BlockSpec((tb, LANES), row),
                  pl.BlockSpec((CONV_W, CONV_DIM), const),
                  pl.BlockSpec((1, LANES), const), pl.BlockSpec((1, LANES), const),
                  pl.BlockSpec((1, VW_B), const)],
        out_specs=[pl.BlockSpec((tb, VW_B), row),
                   pl.BlockSpec((1, HV_B, DK_B, DV_B), lambda b, t: (b, 0, 0, 0)),
                   pl.BlockSpec((1, CONV_W - 1, CONV_DIM), lambda b, t: (b, 0, 0))],
        out_shape=[jax.ShapeDtypeStruct((batch * seq, VW_B), BF16),
                   jax.ShapeDtypeStruct((batch, HV_B, DK_B, DV_B), F32),
                   jax.ShapeDtypeStruct((batch, CONV_W - 1, CONV_DIM), F32)],
        scratch_shapes=[pltpu.VMEM((tb + CONV_HALO, CONV_DIM), F32),
                        pltpu.VMEM((HV_B, DK_B, DV_B), F32)]
        + _gdn_scratch(1, CHUNK),
        compiler_params=_params("parallel", "arbitrary"),
        name="gdn_prompt",
    )(qkv, z, ba, cw, dtb, alog, nw)


def _gdn_sample_kernel(xp_ref, z_ref, ba_ref, s0_ref, cw_ref, dtb_ref, alog_ref, nw_ref, carry_ref,
                       o_ref, s_out, *scratch, nseq, length, valid):
    del carry_ref
    seqs = []
    for j in range(nseq):
        rows = slice(j * length, (j + 1) * length)

        def conv_rows(col0, j=j):
            return xp_ref[j, :, col0:col0 + LANES]

        def write_o(cols, val, rows=rows):
            o_ref[rows, cols] = val.astype(o_ref.dtype)
        seqs.append((conv_rows, z_ref, ba_ref, rows, write_o))
    _gdn_chunk(seqs, cw_ref, dtb_ref, alog_ref, nw_ref, s0_ref, s_out, scratch, length, valid)


def _gdn_sample(xp, z, ba, s_all, layer, s_carry, cw, dtb, alog, nw, length, valid):
    batch = xp.shape[0]
    nseq = GDN_SAMPLE_SEQS if batch % GDN_SAMPLE_SEQS == 0 else 1
    rows = nseq * length
    row = lambda b: (b, 0)
    const = lambda b: (0, 0)
    s_spec = pl.BlockSpec((None, nseq, HV_B, DK_B, DV_B), lambda b: (layer, b, 0, 0, 0))
    args = [xp, z, ba, s_all, cw, dtb, alog, nw]
    in_specs = [pl.BlockSpec((nseq,) + xp.shape[1:], lambda b: (b, 0, 0)),
                pl.BlockSpec((rows, VW_B), row),
                pl.BlockSpec((rows, LANES), row),
                s_spec,
                pl.BlockSpec((CONV_W, CONV_DIM), const),
                pl.BlockSpec((1, LANES), const), pl.BlockSpec((1, LANES), const),
                pl.BlockSpec((1, VW_B), const)]
    aliases = {len(args): 1}
    args.append(s_carry)
    in_specs.append(pl.BlockSpec(memory_space=pl.ANY))
    return pl.pallas_call(
        functools.partial(_gdn_sample_kernel, nseq=nseq, length=length, valid=valid),
        grid=(batch // nseq,),
        in_specs=in_specs,
        out_specs=[pl.BlockSpec((rows, VW_B), row), s_spec],
        out_shape=[jax.ShapeDtypeStruct((batch * length, VW_B), BF16),
                   jax.ShapeDtypeStruct(s_all.shape, F32)],
        scratch_shapes=_gdn_scratch(nseq, length),
        input_output_aliases=aliases,
        compiler_params=_params("arbitrary"),
        name="gdn_sample",
    )(*args)


def _pad_cols(w, width):
    return jnp.pad(w, ((0, 0), (0, width - w.shape[1])))


def _lane_row(vec, offset):
    return jnp.zeros((1, LANES), F32).at[0, offset:offset + vec.shape[0]].set(vec.astype(F32))


MLSTM_SEGS = ((0, QK_A), (QK_A, QK_A), (2 * QK_A, VW_A), (2 * QK_A + VW_A, VW_A), (2 * QK_A + 2 * VW_A, LANES))
GDN_SEGS = ((0, CONV_DIM), (CONV_DIM, VW_B), (CONV_DIM + VW_B, LANES))


def _mlstm_layer(x, batch, seq, s_batch, s_seq, w_in, gate_b, norm_w, w_out, c_all, layer, c_carry, n0, m0,
                 ln_g, ln_b):
    w = _pad_cols(w_in, MLSTM_SEGS[-1][0] + LANES).astype(BF16)
    q, k, v, o, g = _proj(x, w, MLSTM_SEGS, (BF16, BF16, BF16, BF16, F32))
    gb = _lane_row(gate_b, 0)
    nw = norm_w.astype(F32).reshape(1, VW_A)
    hg_p, c_p, n_p, m_p = _mlstm_prompt(q, k, v, o, g, gb, nw, batch, seq)

    def padded(a):
        a = a[batch * seq:].reshape(s_batch, s_seq, a.shape[1])
        return jnp.pad(a, ((0, 0), (0, SAMPLE_CHUNK - s_seq), (0, 0))).reshape(s_batch * SAMPLE_CHUNK, a.shape[2])

    m0_lanes = jnp.pad(m0.astype(F32), ((0, 0), (H_A, LANES - 2 * H_A))).reshape(s_batch, 1, LANES)
    hg_s, c_s, n_s, m_s = _mlstm_sample(padded(q), padded(k), padded(v), padded(o), padded(g), c_all, layer,
                                        c_carry, n0, m0_lanes, gb, nw, SAMPLE_CHUNK, s_seq)
    hg_s = hg_s.reshape(s_batch, SAMPLE_CHUNK, VW_A)[:, :s_seq].reshape(s_batch * s_seq, VW_A)
    x = _outproj_ln((hg_p, hg_s), x, w_out.astype(BF16), ln_g.reshape(1, -1), ln_b.reshape(1, -1))
    return x, (c_p, n_p, m_p[:, 0, H_A:2 * H_A]), (c_s, n_s, m_s[:, 0, H_A:2 * H_A])


def _gdn_layer(x, batch, seq, s_batch, s_seq, w_in, conv_w, dt_bias, a_log, norm_w, w_out,
               s_all, layer, s_carry, conv0, ln_g, ln_b):
    w = _pad_cols(w_in, GDN_SEGS[-1][0] + LANES).astype(BF16)
    qkv, z, ba = _proj(x, w, GDN_SEGS, (BF16, BF16, F32))
    cw = conv_w.astype(F32)
    dtb = _lane_row(dt_bias, HV_B)
    alog = _lane_row(a_log, HV_B)
    nw = norm_w.astype(F32).reshape(1, VW_B)
    o_p, s_p, conv_p = _gdn_prompt(qkv, z, ba, cw, dtb, alog, nw, batch, seq)

    def tail(a):
        return a[batch * seq:].reshape(s_batch, s_seq, a.shape[1])

    def padded(a):
        return jnp.pad(tail(a), ((0, 0), (0, SAMPLE_CHUNK - s_seq), (0, 0))).reshape(s_batch * SAMPLE_CHUNK, a.shape[1])

    xp = jnp.concatenate([conv0.astype(F32), tail(qkv).astype(F32)], axis=1)
    conv_s = xp[:, s_seq:]
    front = CONV_HALO - (CONV_W - 1)
    xp = jnp.pad(xp, ((0, 0), (front, SAMPLE_CHUNK - s_seq), (0, 0)))
    o_s, s_s = _gdn_sample(xp, padded(z), padded(ba), s_all, layer, s_carry, cw, dtb, alog, nw,
                           SAMPLE_CHUNK, s_seq)
    o_s = o_s.reshape(s_batch, SAMPLE_CHUNK, VW_B)[:, :s_seq].reshape(s_batch * s_seq, VW_B)
    x = _outproj_ln((o_p, o_s), x, w_out.astype(BF16), ln_g.reshape(1, -1), ln_b.reshape(1, -1))
    return x, (s_p, conv_p), (s_s, conv_s)


def kernel(x_prompt, x_sample, state_mlstm_C, state_mlstm_n, state_mlstm_m, state_gdn_S, state_gdn_conv,
           a_w_in, a_gate_b, a_norm_w, a_w_out, b_w_in, b_conv_w, b_dt_bias, b_a_log, b_norm_w, b_w_out,
           mlp_w1, mlp_w2, ln1_g, ln1_b, ln2_g, ln2_b):
    batch, seq, d = x_prompt.shape
    s_batch, s_seq, _ = x_sample.shape
    n_prompt = batch * seq
    n_sample = s_batch * s_seq
    assert n_prompt % ROW_TILE == 0 and n_sample == ROW_TILE and seq % CHUNK == 0 and s_seq <= SAMPLE_CHUNK
    x = (x_prompt.reshape(n_prompt, d), x_sample.reshape(n_sample, d))
    p_a, s_a, p_b, s_b = [], [], [], []
    c_carry = jnp.zeros(state_mlstm_C.shape, F32)
    s_carry = jnp.zeros(state_gdn_S.shape, F32)
    for layer in range(DEPTH):
        j = layer // 2
        if layer % 2 == 0:
            x, p_state, s_state = _mlstm_layer(
                x, batch, seq, s_batch, s_seq, a_w_in[j], a_gate_b[j], a_norm_w[j], a_w_out[j],
                state_mlstm_C, j, c_carry, state_mlstm_n[j], state_mlstm_m[j], ln1_g[layer], ln1_b[layer])
            c_carry = s_state[0]
            p_a.append(p_state)
            s_a.append(s_state)
        else:
            x, p_state, s_state = _gdn_layer(
                x, batch, seq, s_batch, s_seq, b_w_in[j], b_conv_w[j], b_dt_bias[j], b_a_log[j],
                b_norm_w[j], b_w_out[j], state_gdn_S, j, s_carry, state_gdn_conv[j], ln1_g[layer], ln1_b[layer])
            s_carry = s_state[0]
            p_b.append(p_state)
            s_b.append(s_state)
        x = _mlp_ln(x, mlp_w1[layer].astype(BF16), mlp_w2[layer].astype(BF16),
                    ln2_g[layer].reshape(1, -1), ln2_b[layer].reshape(1, -1),
                    split_rows=n_prompt if layer == DEPTH - 1 else None)
    y_prompt = x[0].reshape(batch, seq, d)
    y_sample = x[1].reshape(s_batch, s_seq, d)

    def stack(states, i):
        return jnp.stack([s[i] for s in states])

    return (y_prompt, y_sample,
            stack(p_a, 0), stack(p_a, 1), stack(p_a, 2), stack(p_b, 0), stack(p_b, 1),
            c_carry, stack(s_a, 1), stack(s_a, 2), s_carry, stack(s_b, 1))
```

```python
import functools

import jax
import jax.numpy as jnp
from jax import lax
from jax.experimental import pallas as pl
from jax.experimental.pallas import tpu as pltpu

F32 = jnp.float32
BF16 = jnp.bfloat16

D_MODEL = 1024
DEPTH = 4
H_A = 8
DV_A = 128
DK_A = 64
QK_A = H_A * DK_A
VW_A = H_A * DV_A
GATE_CAP = 15.0
HK_B = 8
HV_B = 16
DK_B = 128
DV_B = 128
QK_B = HK_B * DK_B
VW_B = HV_B * DV_B
CONV_W = 4
CONV_DIM = 2 * QK_B + VW_B
CHUNK = 64
D_FF = 4 * D_MODEL
ALPHA = (2.0 * DEPTH) ** 0.25
LN_EPS = 1e-5
RMS_EPS = 1e-6

LANES = 128
SUBLANES = 8
VMEM_LIMIT = 56 * 1024 * 1024
ROW_TILE = 512
FF_TILE = 1024
MLSTM_ROWS = 512
MLSTM_SEQS = 1
MLSTM_SAMPLE_SEQS = 4
GDN_SAMPLE_SEQS = 2
SAMPLE_CHUNK = 16
GDN_ROWS = 256

NEG_INF = float("-inf")


def _params(*sem):
    return pltpu.CompilerParams(dimension_semantics=sem, vmem_limit_bytes=VMEM_LIMIT)


def _dot(a, b):
    return jnp.dot(a, b, preferred_element_type=F32)


def _dot_nt(a, b):
    return lax.dot_general(a, b, (((1,), (1,)), ((), ())), preferred_element_type=F32)


def _dot_tn(a, b):
    return lax.dot_general(a, b, (((0,), (0,)), ((), ())), preferred_element_type=F32)


def _split2(x):
    hi = x.astype(BF16)
    lo = (x - hi.astype(F32)).astype(BF16)
    return hi, lo


def _dot_split(a, b):
    ah, al = _split2(a)
    bh, bl = _split2(b)
    return _dot(ah, bh) + (_dot(ah, bl) + _dot(al, bh))


def _cumsum_rows(tri_bf, x):
    x1 = x.astype(BF16)
    r1 = x - x1.astype(F32)
    x2 = r1.astype(BF16)
    x3 = (r1 - x2.astype(F32)).astype(BF16)
    return _dot(tri_bf, x1) + (_dot(tri_bf, x2) + _dot(tri_bf, x3))


def _layer_norm(r, g, b):
    mu = jnp.mean(r, axis=-1, keepdims=True)
    d = r - mu
    var = jnp.mean(d * d, axis=-1, keepdims=True)
    return d * lax.rsqrt(var + LN_EPS) * g + b


def _rms(h):
    return h * lax.rsqrt(jnp.mean(h * h, axis=-1, keepdims=True) + RMS_EPS)


def _l2norm(x):
    return x * lax.rsqrt(jnp.sum(x * x, axis=-1, keepdims=True) + RMS_EPS)


def _sigmoid(x):
    return 0.5 * jnp.tanh(0.5 * x) + 0.5


def _rows(x):
    return x[0].shape[0] + x[1].shape[0] if isinstance(x, tuple) else x.shape[0]


def _row_args(x):
    return list(x) if isinstance(x, tuple) else [x]


def _row_specs(x):
    if isinstance(x, tuple):
        main, tail = x
        last = main.shape[0] // ROW_TILE - 1
        return [pl.BlockSpec((ROW_TILE, main.shape[1]), lambda i: (jnp.minimum(i, last), 0)),
                pl.BlockSpec((ROW_TILE, tail.shape[1]), lambda i: (0, 0))]
    return [pl.BlockSpec((ROW_TILE, x.shape[1]), lambda i: (i, 0))]


def _row_load(refs):
    if len(refs) == 1:
        return refs[0][...]
    on_tail = pl.program_id(0) == pl.num_programs(0) - 1
    return jnp.where(on_tail, refs[1][...], refs[0][...])


def _proj_kernel(*refs, segs, nx):
    xb = _row_load(refs[:nx]).astype(BF16)
    w_ref = refs[nx]
    out_refs = refs[nx + 1:]
    for (start, width), o_ref in zip(segs, out_refs):
        for c0 in range(0, width, FF_TILE):
            cw = min(FF_TILE, width - c0)
            y = _dot(xb, w_ref[:, start + c0:start + c0 + cw])
            o_ref[:, c0:c0 + cw] = y.astype(o_ref.dtype)


def _proj(x, w, segs, dtypes):
    n = _rows(x)
    args = _row_args(x)
    return pl.pallas_call(
        functools.partial(_proj_kernel, segs=segs, nx=len(args)),
        grid=(n // ROW_TILE,),
        in_specs=_row_specs(x) + [pl.BlockSpec(w.shape, lambda i: (0, 0))],
        out_specs=[pl.BlockSpec((ROW_TILE, wd), lambda i: (i, 0)) for _, wd in segs],
        out_shape=[jax.ShapeDtypeStruct((n, wd), dt) for (_, wd), dt in zip(segs, dtypes)],
        compiler_params=_params("arbitrary"),
        name="proj",
    )(*args, w)


def _outproj_ln_kernel(*refs, nh, nx):
    w_ref, g_ref, b_ref, o_ref = refs[nh + nx:]
    y = _dot(_row_load(refs[:nh]), w_ref[...])
    o_ref[...] = _layer_norm(ALPHA * _row_load(refs[nh:nh + nx]) + y, g_ref[...], b_ref[...])


def _outproj_ln(h, x, w, g, b):
    kh, d = w.shape
    n = _rows(x)
    const = lambda i: (0, 0)
    h_args, x_args = _row_args(h), _row_args(x)
    return pl.pallas_call(
        functools.partial(_outproj_ln_kernel, nh=len(h_args), nx=len(x_args)),
        grid=(n // ROW_TILE,),
        in_specs=_row_specs(h) + _row_specs(x) + [pl.BlockSpec((kh, d), const),
                                                  pl.BlockSpec((1, d), const), pl.BlockSpec((1, d), const)],
        out_specs=pl.BlockSpec((ROW_TILE, d), lambda i: (i, 0)),
        out_shape=jax.ShapeDtypeStruct((n, d), F32),
        compiler_params=_params("arbitrary"),
        name="outproj_ln",
    )(*h_args, *x_args, w, g, b)


def _stage_weights_bf16(w1_hbm, w2_hbm, layer, w1_ref, w2_ref, stage, sem):
    d, dff = w1_ref.shape
    step = stage.shape[1]
    assert stage.shape == (2, step, step) and d == step and dff % step == 0
    plan = ([(w1_hbm.at[layer, :, c:c + step], w1_ref.at[:, c:c + step]) for c in range(0, dff, step)]
            + [(w2_hbm.at[layer, c:c + step, :], w2_ref.at[c:c + step, :]) for c in range(0, dff, step)])
    copies = [pltpu.make_async_copy(src, stage.at[i % 2], sem.at[i % 2]) for i, (src, _) in enumerate(plan)]
    copies[0].start()
    for i, (_, dst) in enumerate(plan):
        if i + 1 < len(plan):
            copies[i + 1].start()
        copies[i].wait()
        dst[...] = stage[i % 2].astype(BF16)


def _mlp_ln_kernel(x_ref, w1_hbm, w2_hbm, g_ref, b_ref, *out_and_scratch, split, layer):
    acc_ref, w1_ref, w2_ref, stage, sem = out_and_scratch[-5:]

    @pl.when(pl.program_id(0) == 0)
    def _():
        _stage_weights_bf16(w1_hbm, w2_hbm, layer, w1_ref, w2_ref, stage, sem)

    x = x_ref[...]
    xb = x.astype(BF16)
    dff = w1_ref.shape[1]
    for c0 in range(0, dff, FF_TILE):
        h = _dot(xb, w1_ref[:, c0:c0 + FF_TILE])
        h = jnp.square(jnp.maximum(h, 0.0)).astype(BF16)
        y = _dot(h, w2_ref[c0:c0 + FF_TILE, :])
        if c0 == 0:
            acc_ref[...] = y
        else:
            acc_ref[...] += y
    res = _layer_norm(ALPHA * x + acc_ref[...], g_ref[...], b_ref[...])
    if not split:
        out_and_scratch[0][...] = res
        return
    main_ref, tail_ref = out_and_scratch[:2]
    on_tail = pl.program_id(0) == pl.num_programs(0) - 1

    @pl.when(jnp.logical_not(on_tail))
    def _():
        main_ref[...] = res

    @pl.when(on_tail)
    def _():
        tail_ref[...] = res


def _mlp_ln(x, w1_all, w2_all, layer, g, b, split_rows=None):
    n = x.shape[0]
    _, d, dff = w1_all.shape
    const = lambda i: (0, 0)
    if split_rows is None:
        out_specs = pl.BlockSpec((ROW_TILE, d), lambda i: (i, 0))
        out_shape = jax.ShapeDtypeStruct((n, d), F32)
    else:
        assert n - split_rows == ROW_TILE
        last = split_rows // ROW_TILE - 1
        out_specs = [pl.BlockSpec((ROW_TILE, d), lambda i: (jnp.minimum(i, last), 0)),
                     pl.BlockSpec((ROW_TILE, d), lambda i: (0, 0))]
        out_shape = [jax.ShapeDtypeStruct((split_rows, d), F32), jax.ShapeDtypeStruct((ROW_TILE, d), F32)]
    return pl.pallas_call(
        functools.partial(_mlp_ln_kernel, split=split_rows is not None, layer=layer),
        grid=(n // ROW_TILE,),
        in_specs=[pl.BlockSpec((ROW_TILE, d), lambda i: (i, 0)),
                  pl.BlockSpec(memory_space=pl.ANY), pl.BlockSpec(memory_space=pl.ANY),
                  pl.BlockSpec((1, d), const), pl.BlockSpec((1, d), const)],
        out_specs=out_specs,
        out_shape=out_shape,
        scratch_shapes=[pltpu.VMEM((ROW_TILE, d), F32),
                        pltpu.VMEM((d, dff), BF16),
                        pltpu.VMEM((dff, d), BF16),
                        pltpu.VMEM((2, d, d), F32),
                        pltpu.SemaphoreType.DMA((2,))],
        compiler_params=_params("arbitrary"),
        name="mlp_ln",
    )(x, w1_all, w2_all, g, b)


def _tri_masks(n):
    row = lax.broadcasted_iota(jnp.int32, (n, n), 0)
    col = lax.broadcasted_iota(jnp.int32, (n, n), 1)
    return row >= col, row > col, row == col


def _mlstm_gates(g_raw, gb):
    cap = GATE_CAP * jnp.tanh((g_raw + gb) / GATE_CAP)
    return cap, jax.nn.log_sigmoid(cap)


def _mlstm_gate_block(g_raw, gb, gate_ref, length, valid):
    rows = g_raw.shape[0]
    cap, lf = _mlstm_gates(g_raw, gb)
    pos = lax.broadcasted_iota(jnp.int32, (rows, LANES), 0) % length
    lane_id = lax.broadcasted_iota(jnp.int32, (rows, LANES), 1)
    head_lanes = (lane_id >= H_A) & (lane_id < 2 * H_A)
    i_sh = pltpu.roll(cap, H_A, axis=1)
    if valid < length:
        i_sh = jnp.where(pos < valid, i_sh, NEG_INF)
        lf = jnp.where(pos < valid, lf, 0.0)
    i_sh = jnp.where(head_lanes, i_sh, 0.0)
    r = lax.broadcasted_iota(jnp.int32, (rows, rows), 0)
    c = lax.broadcasted_iota(jnp.int32, (rows, rows), 1)
    tri_bf = ((r >= c) & (r // length == c // length)).astype(BF16)
    bcum = jnp.where(head_lanes, _cumsum_rows(tri_bf, lf), 0.0)
    cmax = i_sh - bcum
    span = 1
    while span < length:
        cmax = jnp.maximum(cmax, jnp.where(pos >= span, pltpu.roll(cmax, span, axis=0), NEG_INF))
        span *= 2
    gate_ref[0] = i_sh
    gate_ref[1] = bcum
    gate_ref[2] = cmax


def _mlstm_chunk(seqs, nw_ref, state_in, state_out, scratch, length):
    c_in, nb_in, m_in = state_in
    c_o, nb_o, m_o = state_out
    q_s, k_s, vx_s, e_s, kwt_s, wi_s, em_s, dc_s = scratch
    nseq = len(seqs)
    groups = nseq * H_A
    causal, _, _ = _tri_masks(length)
    scale = DK_A ** -0.5
    ones = jnp.ones((length, DV_A), BF16)

    for j, (q_ref, k_ref, v_ref, _, gate_ref, rows, _) in enumerate(seqs):
        i_sh = gate_ref[0, rows, :]
        bcum = gate_ref[1, rows, :]
        cmax = gate_ref[2, rows, :]
        cmat = i_sh - bcum
        m_prev = m_in[j]
        m_t = bcum + jnp.maximum(m_prev, cmax)
        a_mat = bcum - m_t
        w_inter = jnp.exp(bcum + m_prev - m_t)
        e_m = jnp.exp(-m_t)
        m_new = m_t[length - 1:length, :]
        b_last = bcum[length - 1:length, :]
        w_k = jnp.exp(b_last - bcum + i_sh - m_new)
        decay = jnp.exp(b_last + m_prev - m_new)
        m_o[j] = m_new
        cmat_t = cmat.T
        for h in range(H_A):
            g = j * H_A + h
            lane = H_A + h
            q_s[g] = q_ref[rows, h * DK_A:(h + 1) * DK_A]
            kh = k_ref[rows, h * DK_A:(h + 1) * DK_A]
            k_s[g] = kh
            vx_s[g, :, 0:DV_A] = v_ref[rows, h * DV_A:(h + 1) * DV_A]
            vx_s[g, :, DV_A:2 * DV_A] = ones
            e_s[g] = jnp.exp(jnp.where(causal, a_mat[:, lane:lane + 1] + cmat_t[lane:lane + 1, :], NEG_INF))
            wi_s[g] = jnp.broadcast_to(w_inter[:, lane:lane + 1], (length, DV_A))
            em_s[g] = jnp.broadcast_to(e_m[:, lane:lane + 1], (length, DV_A))
            kwt_s[g] = (kh.astype(F32) * scale * w_k[:, lane:lane + 1]).T.astype(BF16)
            dc_s[g] = jnp.broadcast_to(decay[:, lane:lane + 1], (1, DV_A))
    qb = q_s[...]
    qk = _bdot_nt(qb, k_s[...])
    c_old = c_in[...].reshape(groups, DK_A, DV_A)
    nb_old = nb_in[...]
    qcx = _bdot(qb, jnp.concatenate([c_old, nb_old], axis=-1).astype(BF16))
    sw = (qk * scale * e_s[...]).astype(BF16)
    vx = vx_s[...]
    svx = _bdot(sw, vx)
    kvx = _bdot(kwt_s[...], vx)
    wi = wi_s[...]
    num = svx[:, :, 0:DV_A] + wi * qcx[:, :, 0:DV_A]
    den = svx[:, :, DV_A:2 * DV_A] + wi * qcx[:, :, DV_A:2 * DV_A]
    hh = num / jnp.maximum(jnp.abs(den), em_s[...])
    dc = dc_s[...]
    c_o[...] = (dc * c_old + kvx[:, :, 0:DV_A]).reshape(c_o.shape)
    nb_o[...] = dc * nb_old + kvx[:, :, DV_A:2 * DV_A]
    hn = _rms(hh) * nw_ref[...]
    for j, (_, _, _, o_ref, _, rows, write_hg) in enumerate(seqs):
        for h in range(H_A):
            cols = slice(h * DV_A, (h + 1) * DV_A)
            gate = _sigmoid(o_ref[rows, cols].astype(F32))
            write_hg(cols, gate * hn[j * H_A + h])


def _n_to_lanes(n_row):
    return jnp.broadcast_to(n_row, (LANES, DK_A)).T


def _n_from_lanes(nb):
    return nb.T[0:1, :]


def _mlstm_scratch(groups, length):
    return [pltpu.VMEM((groups, length, DK_A), BF16),
            pltpu.VMEM((groups, length, DK_A), BF16),
            pltpu.VMEM((groups, length, 2 * DV_A), BF16),
            pltpu.VMEM((groups, length, length), F32),
            pltpu.VMEM((groups, DK_A, length), BF16),
            pltpu.VMEM((groups, length, DV_A), F32),
            pltpu.VMEM((groups, length, DV_A), F32),
            pltpu.VMEM((groups, 1, DV_A), F32)]


def _mlstm_prompt_kernel(*refs, nseq):
    ins = refs[:5 * nseq]
    gb_ref, nw_ref, hg_ref, c_out, n_out, m_out, c_s, n_s, m_s, gate_s = refs[5 * nseq:5 * nseq + 10]
    scratch = refs[5 * nseq + 10:]
    t = pl.program_id(1)

    @pl.when(t == 0)
    def _():
        c_s[...] = jnp.zeros_like(c_s)
        n_s[...] = jnp.zeros_like(n_s)
        m_s[...] = jnp.zeros_like(m_s)

    for j in range(nseq):
        _mlstm_gate_block(ins[4 * nseq + j][...], gb_ref[...], gate_s.at[j], CHUNK, CHUNK)

    def chunk(c, carry):
        rows = pl.ds(pl.multiple_of(c * CHUNK, CHUNK), CHUNK)
        seqs = []
        for j in range(nseq):
            def write_hg(cols, val, j=j):
                hg_ref[j, rows, cols] = val.astype(hg_ref.dtype)
            seqs.append(tuple(ins[i * nseq + j] for i in range(4)) + (gate_s.at[j], rows, write_hg))
        state = (c_s, n_s, m_s)
        _mlstm_chunk(seqs, nw_ref, state, state, scratch, CHUNK)
        return carry

    lax.fori_loop(0, hg_ref.shape[1] // CHUNK, chunk, 0)

    @pl.when(t == pl.num_programs(1) - 1)
    def _():
        c_out[...] = c_s[...].reshape(c_out.shape)
        for j in range(nseq):
            for h in range(H_A):
                n_out[j, h:h + 1, :] = _n_from_lanes(n_s[j * H_A + h])
        m_out[...] = m_s[...]


def _mlstm_prompt(q, k, v, o, g, gb, nw, batch, seq):
    tb = min(MLSTM_ROWS, seq)
    nt = seq // tb
    nseq = MLSTM_SEQS if batch % MLSTM_SEQS == 0 else 1
    groups = nseq * H_A
    const = lambda b, t: (0, 0)

    def rows_of(j):
        return lambda b, t: ((b * nseq + j) * nt + t, 0)

    in_specs, args = [], []
    for arr, width in ((q, QK_A), (k, QK_A), (v, VW_A), (o, VW_A), (g, LANES)):
        for j in range(nseq):
            in_specs.append(pl.BlockSpec((tb, width), rows_of(j)))
            args.append(arr)
    in_specs += [pl.BlockSpec((1, LANES), const), pl.BlockSpec((groups, 1, DV_A), lambda b, t: (0, 0, 0))]
    args += [gb, jnp.tile(nw.reshape(H_A, 1, DV_A), (nseq, 1, 1))]
    hg, c, n, m = pl.pallas_call(
        functools.partial(_mlstm_prompt_kernel, nseq=nseq),
        grid=(batch // nseq, nt),
        in_specs=in_specs,
        out_specs=[pl.BlockSpec((nseq, tb, VW_A), lambda b, t: (b, t, 0)),
                   pl.BlockSpec((nseq, H_A, DK_A, DV_A), lambda b, t: (b, 0, 0, 0)),
                   pl.BlockSpec((nseq, H_A, DK_A), lambda b, t: (b, 0, 0)),
                   pl.BlockSpec((nseq, 1, LANES), lambda b, t: (b, 0, 0))],
        out_shape=[jax.ShapeDtypeStruct((batch, seq, VW_A), BF16),
                   jax.ShapeDtypeStruct((batch, H_A, DK_A, DV_A), F32),
                   jax.ShapeDtypeStruct((batch, H_A, DK_A), F32),
                   jax.ShapeDtypeStruct((batch, 1, LANES), F32)],
        scratch_shapes=[pltpu.VMEM((groups, DK_A, DV_A), F32),
                        pltpu.VMEM((groups, DK_A, LANES), F32),
                        pltpu.VMEM((nseq, 1, LANES), F32),
                        pltpu.VMEM((nseq, 3, tb, LANES), F32)]
        + _mlstm_scratch(groups, CHUNK),
        compiler_params=_params("parallel", "arbitrary"),
        name="mlstm_prompt",
    )(*args)
    return hg.reshape(batch * seq, VW_A), c, n, m


def _mlstm_sample_kernel(q_ref, k_ref, v_ref, o_ref, g_ref, c0_ref, n0_ref, m0_ref, gb_ref, nw_ref, carry_ref,
                         hg_ref, c_out, n_out, m_out, nb_s, gate_s, *scratch, nseq, length, valid):
    del carry_ref
    _mlstm_gate_block(g_ref[...], gb_ref[...], gate_s, length, valid)
    seqs = []
    for j in range(nseq):
        rows = slice(j * length, (j + 1) * length)

        def write_hg(cols, val, rows=rows):
            hg_ref[rows, cols] = val.astype(hg_ref.dtype)
        seqs.append((q_ref, k_ref, v_ref, o_ref, gate_s, rows, write_hg))
        for h in range(H_A):
            nb_s[j * H_A + h] = _n_to_lanes(n0_ref[j, h:h + 1, :])
    _mlstm_chunk(seqs, nw_ref, (c0_ref, nb_s, m0_ref), (c_out, nb_s, m_out), scratch, length)
    for j in range(nseq):
        for h in range(H_A):
            n_out[j, h:h + 1, :] = _n_from_lanes(nb_s[j * H_A + h])


def _mlstm_sample(q, k, v, o, g, c_all, layer, c_carry, n0, m0, gb, nw, length, valid):
    batch = n0.shape[0]
    nseq = MLSTM_SAMPLE_SEQS if batch % MLSTM_SAMPLE_SEQS == 0 else 1
    groups = nseq * H_A
    rows = nseq * length
    row = lambda b: (b, 0)
    b3 = lambda b: (b, 0, 0)
    c_spec = pl.BlockSpec((None, nseq, H_A, DK_A, DV_A), lambda b: (layer, b, 0, 0, 0))
    args = [q, k, v, o, g, c_all, n0, m0, gb, jnp.tile(nw.reshape(H_A, 1, DV_A), (nseq, 1, 1))]
    in_specs = [pl.BlockSpec((rows, QK_A), row), pl.BlockSpec((rows, QK_A), row),
                pl.BlockSpec((rows, VW_A), row), pl.BlockSpec((rows, VW_A), row),
                pl.BlockSpec((rows, LANES), row),
                c_spec,
                pl.BlockSpec((nseq, H_A, DK_A), b3),
                pl.BlockSpec((nseq, 1, LANES), b3),
                pl.BlockSpec((1, LANES), lambda b: (0, 0)),
                pl.BlockSpec((groups, 1, DV_A), lambda b: (0, 0, 0))]
    aliases = {len(args): 1}
    args.append(c_carry)
    in_specs.append(pl.BlockSpec(memory_space=pl.ANY))
    return pl.pallas_call(
        functools.partial(_mlstm_sample_kernel, nseq=nseq, length=length, valid=valid),
        grid=(batch // nseq,),
        in_specs=in_specs,
        out_specs=[pl.BlockSpec((rows, VW_A), row),
                   c_spec,
                   pl.BlockSpec((nseq, H_A, DK_A), b3),
                   pl.BlockSpec((nseq, 1, LANES), b3)],
        out_shape=[jax.ShapeDtypeStruct((batch * length, VW_A), BF16),
                   jax.ShapeDtypeStruct(c_all.shape, F32),
                   jax.ShapeDtypeStruct((batch, H_A, DK_A), F32),
                   jax.ShapeDtypeStruct((batch, 1, LANES), F32)],
        scratch_shapes=[pltpu.VMEM((groups, DK_A, LANES), F32),
                        pltpu.VMEM((3, rows, LANES), F32)]
        + _mlstm_scratch(groups, length),
        input_output_aliases=aliases,
        compiler_params=_params("arbitrary"),
        name="mlstm_sample",
    )(*args)


def _gdn_gates(ba, alog, dtb):
    beta = _sigmoid(ba)
    g = -jnp.exp(alog) * jax.nn.softplus(ba + dtb)
    return beta, g


def _bdot(a, b):
    return jnp.einsum('hmk,hkn->hmn', a, b, preferred_element_type=F32)


def _bdot_nt(a, b):
    return jnp.einsum('hmk,hnk->hmn', a, b, preferred_element_type=F32)


def _unit_lower_solve(nmat, x, order):
    span = 1
    while span < order:
        nh = nmat.astype(BF16)
        xh, xl = _split2(x)
        x = x + (_bdot(nh, xh) + _bdot(nh, xl))
        span *= 2
        if span < order:
            nmat = _bdot(nh, nh)
    return x


CONV_HALO = SUBLANES


def _gdn_chunk(seqs, cw_ref, dtb_ref, alog_ref, nw_ref, s_in, s_o, scratch, length, valid):
    q_s, k_s, bv_s, beg_s, n_s, attn_s, kq_s, kdt_s, gl_s = scratch
    nseq = len(seqs)
    groups = nseq * HV_B
    tril, strict, _ = _tri_masks(length)
    tri_bf = tril.astype(BF16)
    scale = DK_B ** -0.5
    rep = HV_B // HK_B
    live = lax.broadcasted_iota(jnp.int32, (length, LANES), 0) < valid

    gates = []
    for j, (conv_rows, _, ba_ref, rows, _) in enumerate(seqs):
        beta_all, g_all = _gdn_gates(ba_ref[rows, :], alog_ref[...], dtb_ref[...])
        if valid < length:
            beta_all = jnp.where(live, beta_all, 0.0)
            g_all = jnp.where(live, g_all, 0.0)
        gc = _cumsum_rows(tri_bf, g_all)
        g_last = gc[length - 1:length, :]
        gates.append((beta_all, gc, gc.T, jnp.exp(gc), jnp.exp(g_last - gc), jnp.exp(g_last)))
        for kh in range(HK_B):
            q_s[j * HK_B + kh] = _l2norm(_conv_silu(conv_rows, cw_ref, kh * DK_B)) * scale
            k_s[j * HK_B + kh] = _l2norm(_conv_silu(conv_rows, cw_ref, QK_B + kh * DK_B))
    kb = k_s[...].astype(BF16)
    kk = _bdot_nt(kb, kb)
    qk = _bdot_nt(q_s[...].astype(BF16), kb)

    for j, (conv_rows, _, _, _, _) in enumerate(seqs):
        beta_all, gc, gc_t, eg_all, egd_all, gl_all = gates[j]
        for hv in range(HV_B):
            g = j * HV_B + hv
            kh = j * HK_B + hv // rep
            gl = HV_B + hv
            col0 = 2 * QK_B + hv * DV_B
            v = _conv_silu(conv_rows, cw_ref, col0)
            k = k_s[kh]
            beta = beta_all[:, hv:hv + 1]
            eg = eg_all[:, gl:gl + 1]
            decay = jnp.exp(jnp.where(tril, gc[:, gl:gl + 1] - gc_t[gl:gl + 1, :], NEG_INF))
            n_s[g] = jnp.where(strict, -(kk[kh] * beta) * decay, 0.0)
            bv_s[g] = v * beta
            beg_s[g] = jnp.broadcast_to(beta * eg, (length, DV_B))
            attn_s[g] = (qk[kh] * decay).astype(BF16)
            kq_s[g, 0:length, :] = k.astype(BF16)
            kq_s[g, length:2 * length, :] = (q_s[kh] * eg).astype(BF16)
            kdt_s[g] = (k * egd_all[:, gl:gl + 1]).T.astype(BF16)
            gl_s[g] = jnp.broadcast_to(gl_all[:, gl:gl + 1], (1, LANES))

    s = s_in[...].reshape(groups, DK_B, DV_B)
    ks = _bdot(kq_s[...], s.astype(BF16))
    v_new = _unit_lower_solve(n_s[...], bv_s[...] - beg_s[...] * ks[:, 0:length, :], valid)
    vb = v_new.astype(BF16)
    o = ks[:, length:2 * length, :] + _bdot(attn_s[...], vb)
    s_o[...] = (gl_s[...] * s + _bdot(kdt_s[...], vb)).reshape(s_o.shape)
    for j, (_, z_ref, _, rows, write_o) in enumerate(seqs):
        for hv in range(HV_B):
            cols = slice(hv * DV_B, (hv + 1) * DV_B)
            zz = z_ref[rows, cols].astype(F32)
            write_o(cols, _rms(o[j * HV_B + hv]) * nw_ref[:, cols] * (zz * _sigmoid(zz)))


def _conv_silu(conv_rows, cw_ref, col0):
    cw = cw_ref[:, col0:col0 + LANES]
    y = conv_rows(col0)
    length = y.shape[0] - CONV_HALO
    first = CONV_HALO - (CONV_W - 1)
    acc = y[first:first + length] * cw[0:1]
    for j in range(1, CONV_W):
        acc = acc + y[first + j:first + j + length] * cw[j:j + 1]
    return acc * _sigmoid(acc)


def _gdn_scratch(nseq, length):
    groups = nseq * HV_B
    return [pltpu.VMEM((nseq * HK_B, length, DK_B), F32),
            pltpu.VMEM((nseq * HK_B, length, DK_B), F32),
            pltpu.VMEM((groups, length, DV_B), F32),
            pltpu.VMEM((groups, length, DV_B), F32),
            pltpu.VMEM((groups, length, length), F32),
            pltpu.VMEM((groups, length, length), BF16),
            pltpu.VMEM((groups, 2 * length, DK_B), BF16),
            pltpu.VMEM((groups, DK_B, length), BF16),
            pltpu.VMEM((groups, 1, LANES), F32)]


def _gdn_prompt_kernel(qkv_ref, z_ref, ba_ref, cw_ref, dtb_ref, alog_ref, nw_ref,
                       o_ref, s_out, conv_out, xpad_s, s_s, *scratch):
    t = pl.program_id(1)
    tb = qkv_ref.shape[0]

    @pl.when(t == 0)
    def _():
        s_s[...] = jnp.zeros_like(s_s)
        xpad_s[0:CONV_HALO, :] = jnp.zeros((CONV_HALO, CONV_DIM), F32)

    xpad_s[CONV_HALO:CONV_HALO + tb, :] = qkv_ref[...].astype(F32)

    for c in range(tb // CHUNK):
        rows = slice(c * CHUNK, (c + 1) * CHUNK)

        def conv_rows(col0, c=c):
            return xpad_s[c * CHUNK:(c + 1) * CHUNK + CONV_HALO, col0:col0 + LANES]

        def write_o(cols, val, rows=rows):
            o_ref[rows, cols] = val.astype(o_ref.dtype)

        _gdn_chunk([(conv_rows, z_ref, ba_ref, rows, write_o)], cw_ref, dtb_ref, alog_ref, nw_ref,
                   s_s, s_s, scratch, CHUNK, CHUNK)

    xpad_s[0:CONV_HALO, :] = xpad_s[tb:tb + CONV_HALO, :]

    @pl.when(t == pl.num_programs(1) - 1)
    def _():
        s_out[0] = s_s[...]
        conv_out[0] = xpad_s[tb + CONV_HALO - (CONV_W - 1):tb + CONV_HALO, :]


def _gdn_prompt(qkv, z, ba, cw, dtb, alog, nw, batch, seq):
    tb = min(GDN_ROWS, seq)
    nt = seq // tb
    row = lambda b, t: (b * nt + t, 0)
    const = lambda b, t: (0, 0)
    return pl.pallas_call(
        _gdn_prompt_kernel,
        grid=(batch, nt),
        in_specs=[pl.BlockSpec((tb, CONV_DIM), row), pl.BlockSpec((tb, VW_B), row),
                  pl.BlockSpec((tb, LANES), row),
                  pl.BlockSpec((CONV_W, CONV_DIM), const),
                  pl.BlockSpec((1, LANES), const), pl.BlockSpec((1, LANES), const),
                  pl.BlockSpec((1, VW_B), const)],
        out_specs=[pl.BlockSpec((tb, VW_B), row),
                   pl.BlockSpec((1, HV_B, DK_B, DV_B), lambda b, t: (b, 0, 0, 0)),
                   pl.BlockSpec((1, CONV_W - 1, CONV_DIM), lambda b, t: (b, 0, 0))],
        out_shape=[jax.ShapeDtypeStruct((batch * seq, VW_B), BF16),
                   jax.ShapeDtypeStruct((batch, HV_B, DK_B, DV_B), F32),
                   jax.ShapeDtypeStruct((batch, CONV_W - 1, CONV_DIM), F32)],
        scratch_shapes=[pltpu.VMEM((tb + CONV_HALO, CONV_DIM), F32),
                        pltpu.VMEM((HV_B, DK_B, DV_B), F32)]
        + _gdn_scratch(1, CHUNK),
        compiler_params=_params("parallel", "arbitrary"),
        name="gdn_prompt",
    )(qkv, z, ba, cw, dtb, alog, nw)


def _gdn_sample_kernel(xp_ref, z_ref, ba_ref, s0_ref, cw_ref, dtb_ref, alog_ref, nw_ref, carry_ref,
                       o_ref, s_out, *scratch, nseq, length, valid):
    del carry_ref
    seqs = []
    for j in range(nseq):
        rows = slice(j * length, (j + 1) * length)

        def conv_rows(col0, j=j):
            return xp_ref[j, :, col0:col0 + LANES]

        def write_o(cols, val, rows=rows):
            o_ref[rows, cols] = val.astype(o_ref.dtype)
        seqs.append((conv_rows, z_ref, ba_ref, rows, write_o))
    _gdn_chunk(seqs, cw_ref, dtb_ref, alog_ref, nw_ref, s0_ref, s_out, scratch, length, valid)


def _gdn_sample(xp, z, ba, s_all, layer, s_carry, cw, dtb, alog, nw, length, valid):
    batch = xp.shape[0]
    nseq = GDN_SAMPLE_SEQS if batch % GDN_SAMPLE_SEQS == 0 else 1
    rows = nseq * length
    row = lambda b: (b, 0)
    const = lambda b: (0, 0)
    s_spec = pl.BlockSpec((None, nseq, HV_B, DK_B, DV_B), lambda b: (layer, b, 0, 0, 0))
    args = [xp, z, ba, s_all, cw, dtb, alog, nw]
    in_specs = [pl.BlockSpec((nseq,) + xp.shape[1:], lambda b: (b, 0, 0)),
                pl.BlockSpec((rows, VW_B), row),
                pl.BlockSpec((rows, LANES), row),
                s_spec,
                pl.BlockSpec((CONV_W, CONV_DIM), const),
                pl.BlockSpec((1, LANES), const), pl.BlockSpec((1, LANES), const),
                pl.BlockSpec((1, VW_B), const)]
    aliases = {len(args): 1}
    args.append(s_carry)
    in_specs.append(pl.BlockSpec(memory_space=pl.ANY))
    return pl.pallas_call(
        functools.partial(_gdn_sample_kernel, nseq=nseq, length=length, valid=valid),
        grid=(batch // nseq,),
        in_specs=in_specs,
        out_specs=[pl.BlockSpec((rows, VW_B), row), s_spec],
        out_shape=[jax.ShapeDtypeStruct((batch * length, VW_B), BF16),
                   jax.ShapeDtypeStruct(s_all.shape, F32)],
        scratch_shapes=_gdn_scratch(nseq, length),
        input_output_aliases=aliases,
        compiler_params=_params("arbitrary"),
        name="gdn_sample",
    )(*args)


def _pad_cols(w, width):
    return jnp.pad(w, ((0, 0), (0, width - w.shape[1])))


def _lane_row(vec, offset):
    return jnp.zeros((1, LANES), F32).at[0, offset:offset + vec.shape[0]].set(vec.astype(F32))


MLSTM_SEGS = ((0, QK_A), (QK_A, QK_A), (2 * QK_A, VW_A), (2 * QK_A + VW_A, VW_A), (2 * QK_A + 2 * VW_A, LANES))
GDN_SEGS = ((0, CONV_DIM), (CONV_DIM, VW_B), (CONV_DIM + VW_B, LANES))


def _mlstm_layer(x, batch, seq, s_batch, s_seq, w_in, gate_b, norm_w, w_out, c_all, layer, c_carry, n0, m0,
                 ln_g, ln_b):
    w = _pad_cols(w_in, MLSTM_SEGS[-1][0] + LANES).astype(BF16)
    q, k, v, o, g = _proj(x, w, MLSTM_SEGS, (BF16, BF16, BF16, BF16, F32))
    gb = _lane_row(gate_b, 0)
    nw = norm_w.astype(F32).reshape(1, VW_A)
    hg_p, c_p, n_p, m_p = _mlstm_prompt(q, k, v, o, g, gb, nw, batch, seq)

    def padded(a):
        a = a[batch * seq:].reshape(s_batch, s_seq, a.shape[1])
        return jnp.pad(a, ((0, 0), (0, SAMPLE_CHUNK - s_seq), (0, 0))).reshape(s_batch * SAMPLE_CHUNK, a.shape[2])

    m0_lanes = jnp.pad(m0.astype(F32), ((0, 0), (H_A, LANES - 2 * H_A))).reshape(s_batch, 1, LANES)
    hg_s, c_s, n_s, m_s = _mlstm_sample(padded(q), padded(k), padded(v), padded(o), padded(g), c_all, layer,
                                        c_carry, n0, m0_lanes, gb, nw, SAMPLE_CHUNK, s_seq)
    hg_s = hg_s.reshape(s_batch, SAMPLE_CHUNK, VW_A)[:, :s_seq].reshape(s_batch * s_seq, VW_A)
    x = _outproj_ln((hg_p, hg_s), x, w_out.astype(BF16), ln_g.reshape(1, -1), ln_b.reshape(1, -1))
    return x, (c_p, n_p, m_p[:, 0, H_A:2 * H_A]), (c_s, n_s, m_s[:, 0, H_A:2 * H_A])


def _gdn_layer(x, batch, seq, s_batch, s_seq, w_in, conv_w, dt_bias, a_log, norm_w, w_out,
               s_all, layer, s_carry, conv0, ln_g, ln_b):
    w = _pad_cols(w_in, GDN_SEGS[-1][0] + LANES).astype(BF16)
    qkv, z, ba = _proj(x, w, GDN_SEGS, (BF16, BF16, F32))
    cw = conv_w.astype(F32)
    dtb = _lane_row(dt_bias, HV_B)
    alog = _lane_row(a_log, HV_B)
    nw = norm_w.astype(F32).reshape(1, VW_B)
    o_p, s_p, conv_p = _gdn_prompt(qkv, z, ba, cw, dtb, alog, nw, batch, seq)

    def tail(a):
        return a[batch * seq:].reshape(s_batch, s_seq, a.shape[1])

    def padded(a):
        return jnp.pad(tail(a), ((0, 0), (0, SAMPLE_CHUNK - s_seq), (0, 0))).reshape(s_batch * SAMPLE_CHUNK, a.shape[1])

    xp = jnp.concatenate([conv0.astype(F32), tail(qkv).astype(F32)], axis=1)
    conv_s = xp[:, s_seq:]
    front = CONV_HALO - (CONV_W - 1)
    xp = jnp.pad(xp, ((0, 0), (front, SAMPLE_CHUNK - s_seq), (0, 0)))
    o_s, s_s = _gdn_sample(xp, padded(z), padded(ba), s_all, layer, s_carry, cw, dtb, alog, nw,
                           SAMPLE_CHUNK, s_seq)
    o_s = o_s.reshape(s_batch, SAMPLE_CHUNK, VW_B)[:, :s_seq].reshape(s_batch * s_seq, VW_B)
    x = _outproj_ln((o_p, o_s), x, w_out.astype(BF16), ln_g.reshape(1, -1), ln_b.reshape(1, -1))
    return x, (s_p, conv_p), (s_s, conv_s)


def kernel(x_prompt, x_sample, state_mlstm_C, state_mlstm_n, state_mlstm_m, state_gdn_S, state_gdn_conv,
           a_w_in, a_gate_b, a_norm_w, a_w_out, b_w_in, b_conv_w, b_dt_bias, b_a_log, b_norm_w, b_w_out,
           mlp_w1, mlp_w2, ln1_g, ln1_b, ln2_g, ln2_b):
    batch, seq, d = x_prompt.shape
    s_batch, s_seq, _ = x_sample.shape
    n_prompt = batch * seq
    n_sample = s_batch * s_seq
    assert n_prompt % ROW_TILE == 0 and n_sample == ROW_TILE and seq % CHUNK == 0 and s_seq <= SAMPLE_CHUNK
    x = (x_prompt.reshape(n_prompt, d), x_sample.reshape(n_sample, d))
    p_a, s_a, p_b, s_b = [], [], [], []
    c_carry = jnp.zeros(state_mlstm_C.shape, F32)
    s_carry = jnp.zeros(state_gdn_S.shape, F32)
    for layer in range(DEPTH):
        j = layer // 2
        if layer % 2 == 0:
            x, p_state, s_state = _mlstm_layer(
                x, batch, seq, s_batch, s_seq, a_w_in[j], a_gate_b[j], a_norm_w[j], a_w_out[j],
                state_mlstm_C, j, c_carry, state_mlstm_n[j], state_mlstm_m[j], ln1_g[layer], ln1_b[layer])
            c_carry = s_state[0]
            p_a.append(p_state)
            s_a.append(s_state)
        else:
            x, p_state, s_state = _gdn_layer(
                x, batch, seq, s_batch, s_seq, b_w_in[j], b_conv_w[j], b_dt_bias[j], b_a_log[j],
                b_norm_w[j], b_w_out[j], state_gdn_S, j, s_carry, state_gdn_conv[j], ln1_g[layer], ln1_b[layer])
            s_carry = s_state[0]
            p_b.append(p_state)
            s_b.append(s_state)
        x = _mlp_ln(x, mlp_w1, mlp_w2, layer, ln2_g[layer].reshape(1, -1), ln2_b[layer].reshape(1, -1),
                    split_rows=n_prompt if layer == DEPTH - 1 else None)
    y_prompt = x[0].reshape(batch, seq, d)
    y_sample = x[1].reshape(s_batch, s_seq, d)

    def stack(states, i):
        return jnp.stack([s[i] for s in states])

    return (y_prompt, y_sample,
            stack(p_a, 0), stack(p_a, 1), stack(p_a, 2), stack(p_b, 0), stack(p_b, 1),
            c_carry, stack(s_a, 1), stack(s_a, 2), s_carry, stack(s_b, 1))
```

```python
import functools

import jax
import jax.numpy as jnp
from jax import lax
from jax.experimental import pallas as pl
from jax.experimental.pallas import tpu as pltpu

F32 = jnp.float32
BF16 = jnp.bfloat16

D_MODEL = 1024
DEPTH = 4
H_A = 8
DV_A = 128
DK_A = 64
QK_A = H_A * DK_A
VW_A = H_A * DV_A
GATE_CAP = 15.0
HK_B = 8
HV_B = 16
DK_B = 128
DV_B = 128
QK_B = HK_B * DK_B
VW_B = HV_B * DV_B
CONV_W = 4
CONV_DIM = 2 * QK_B + VW_B
CHUNK = 64
D_FF = 4 * D_MODEL
ALPHA = (2.0 * DEPTH) ** 0.25
LN_EPS = 1e-5
RMS_EPS = 1e-6

LANES = 128
SUBLANES = 8
VMEM_LIMIT = 56 * 1024 * 1024
ROW_TILE = 512
FF_TILE = 1024
MLSTM_ROWS = 512
MLSTM_SEQS = 1
MLSTM_SAMPLE_SEQS = 4
GDN_SAMPLE_SEQS = 4
SAMPLE_CHUNK = 16
GDN_ROWS = 256

NEG_INF = float("-inf")


def _params(*sem):
    return pltpu.CompilerParams(dimension_semantics=sem, vmem_limit_bytes=VMEM_LIMIT)


def _dot(a, b):
    return jnp.dot(a, b, preferred_element_type=F32)


def _dot_nt(a, b):
    return lax.dot_general(a, b, (((1,), (1,)), ((), ())), preferred_element_type=F32)


def _dot_tn(a, b):
    return lax.dot_general(a, b, (((0,), (0,)), ((), ())), preferred_element_type=F32)


def _split2(x):
    hi = x.astype(BF16)
    lo = (x - hi.astype(F32)).astype(BF16)
    return hi, lo


def _dot_split(a, b):
    ah, al = _split2(a)
    bh, bl = _split2(b)
    return _dot(ah, bh) + (_dot(ah, bl) + _dot(al, bh))


def _cumsum_rows(tri_bf, x):
    x1 = x.astype(BF16)
    r1 = x - x1.astype(F32)
    x2 = r1.astype(BF16)
    x3 = (r1 - x2.astype(F32)).astype(BF16)
    return _dot(tri_bf, x1) + (_dot(tri_bf, x2) + _dot(tri_bf, x3))


def _layer_norm(r, g, b):
    mu = jnp.mean(r, axis=-1, keepdims=True)
    d = r - mu
    var = jnp.mean(d * d, axis=-1, keepdims=True)
    return d * lax.rsqrt(var + LN_EPS) * g + b


def _rms(h):
    return h * lax.rsqrt(jnp.mean(h * h, axis=-1, keepdims=True) + RMS_EPS)


def _l2norm(x):
    return x * lax.rsqrt(jnp.sum(x * x, axis=-1, keepdims=True) + RMS_EPS)


def _sigmoid(x):
    return 0.5 * jnp.tanh(0.5 * x) + 0.5


def _rows(x):
    return x[0].shape[0] + x[1].shape[0] if isinstance(x, tuple) else x.shape[0]


def _row_args(x):
    return list(x) if isinstance(x, tuple) else [x]


def _row_specs(x):
    if isinstance(x, tuple):
        main, tail = x
        last = main.shape[0] // ROW_TILE - 1
        return [pl.BlockSpec((ROW_TILE, main.shape[1]), lambda i: (jnp.minimum(i, last), 0)),
                pl.BlockSpec((ROW_TILE, tail.shape[1]), lambda i: (0, 0))]
    return [pl.BlockSpec((ROW_TILE, x.shape[1]), lambda i: (i, 0))]


def _row_load(refs):
    if len(refs) == 1:
        return refs[0][...]
    on_tail = pl.program_id(0) == pl.num_programs(0) - 1
    return jnp.where(on_tail, refs[1][...], refs[0][...])


def _proj_kernel(*refs, segs, nx):
    xb = _row_load(refs[:nx]).astype(BF16)
    w_ref = refs[nx]
    out_refs = refs[nx + 1:]
    for (start, width), o_ref in zip(segs, out_refs):
        for c0 in range(0, width, FF_TILE):
            cw = min(FF_TILE, width - c0)
            y = _dot(xb, w_ref[:, start + c0:start + c0 + cw])
            o_ref[:, c0:c0 + cw] = y.astype(o_ref.dtype)


def _proj(x, w, segs, dtypes):
    n = _rows(x)
    args = _row_args(x)
    return pl.pallas_call(
        functools.partial(_proj_kernel, segs=segs, nx=len(args)),
        grid=(n // ROW_TILE,),
        in_specs=_row_specs(x) + [pl.BlockSpec(w.shape, lambda i: (0, 0))],
        out_specs=[pl.BlockSpec((ROW_TILE, wd), lambda i: (i, 0)) for _, wd in segs],
        out_shape=[jax.ShapeDtypeStruct((n, wd), dt) for (_, wd), dt in zip(segs, dtypes)],
        compiler_params=_params("arbitrary"),
        name="proj",
    )(*args, w)


def _outproj_ln_kernel(*refs, nh, nx):
    w_ref, g_ref, b_ref, o_ref = refs[nh + nx:]
    y = _dot(_row_load(refs[:nh]), w_ref[...])
    o_ref[...] = _layer_norm(ALPHA * _row_load(refs[nh:nh + nx]) + y, g_ref[...], b_ref[...])


def _outproj_ln(h, x, w, g, b):
    kh, d = w.shape
    n = _rows(x)
    const = lambda i: (0, 0)
    h_args, x_args = _row_args(h), _row_args(x)
    return pl.pallas_call(
        functools.partial(_outproj_ln_kernel, nh=len(h_args), nx=len(x_args)),
        grid=(n // ROW_TILE,),
        in_specs=_row_specs(h) + _row_specs(x) + [pl.BlockSpec((kh, d), const),
                                                  pl.BlockSpec((1, d), const), pl.BlockSpec((1, d), const)],
        out_specs=pl.BlockSpec((ROW_TILE, d), lambda i: (i, 0)),
        out_shape=jax.ShapeDtypeStruct((n, d), F32),
        compiler_params=_params("arbitrary"),
        name="outproj_ln",
    )(*h_args, *x_args, w, g, b)


def _stage_weights_bf16(w1_hbm, w2_hbm, layer, w1_ref, w2_ref, stage, sem):
    d, dff = w1_ref.shape
    step = stage.shape[1]
    assert stage.shape == (2, step, step) and d == step and dff % step == 0
    plan = ([(w1_hbm.at[layer, :, c:c + step], w1_ref.at[:, c:c + step]) for c in range(0, dff, step)]
            + [(w2_hbm.at[layer, c:c + step, :], w2_ref.at[c:c + step, :]) for c in range(0, dff, step)])
    copies = [pltpu.make_async_copy(src, stage.at[i % 2], sem.at[i % 2]) for i, (src, _) in enumerate(plan)]
    copies[0].start()
    for i, (_, dst) in enumerate(plan):
        if i + 1 < len(plan):
            copies[i + 1].start()
        copies[i].wait()
        dst[...] = stage[i % 2].astype(BF16)


def _mlp_ln_kernel(x_ref, w1_hbm, w2_hbm, g_ref, b_ref, *out_and_scratch, split, layer):
    acc_ref, w1_ref, w2_ref, stage, sem = out_and_scratch[-5:]

    @pl.when(pl.program_id(0) == 0)
    def _():
        _stage_weights_bf16(w1_hbm, w2_hbm, layer, w1_ref, w2_ref, stage, sem)

    x = x_ref[...]
    xb = x.astype(BF16)
    dff = w1_ref.shape[1]
    for c0 in range(0, dff, FF_TILE):
        h = _dot(xb, w1_ref[:, c0:c0 + FF_TILE])
        h = jnp.square(jnp.maximum(h, 0.0)).astype(BF16)
        y = _dot(h, w2_ref[c0:c0 + FF_TILE, :])
        if c0 == 0:
            acc_ref[...] = y
        else:
            acc_ref[...] += y
    res = _layer_norm(ALPHA * x + acc_ref[...], g_ref[...], b_ref[...])
    if not split:
        out_and_scratch[0][...] = res
        return
    main_ref, tail_ref = out_and_scratch[:2]
    on_tail = pl.program_id(0) == pl.num_programs(0) - 1

    @pl.when(jnp.logical_not(on_tail))
    def _():
        main_ref[...] = res

    @pl.when(on_tail)
    def _():
        tail_ref[...] = res


def _mlp_ln(x, w1_all, w2_all, layer, g, b, split_rows=None):
    n = x.shape[0]
    _, d, dff = w1_all.shape
    const = lambda i: (0, 0)
    if split_rows is None:
        out_specs = pl.BlockSpec((ROW_TILE, d), lambda i: (i, 0))
        out_shape = jax.ShapeDtypeStruct((n, d), F32)
    else:
        assert n - split_rows == ROW_TILE
        last = split_rows // ROW_TILE - 1
        out_specs = [pl.BlockSpec((ROW_TILE, d), lambda i: (jnp.minimum(i, last), 0)),
                     pl.BlockSpec((ROW_TILE, d), lambda i: (0, 0))]
        out_shape = [jax.ShapeDtypeStruct((split_rows, d), F32), jax.ShapeDtypeStruct((ROW_TILE, d), F32)]
    return pl.pallas_call(
        functools.partial(_mlp_ln_kernel, split=split_rows is not None, layer=layer),
        grid=(n // ROW_TILE,),
        in_specs=[pl.BlockSpec((ROW_TILE, d), lambda i: (i, 0)),
                  pl.BlockSpec(memory_space=pl.ANY), pl.BlockSpec(memory_space=pl.ANY),
                  pl.BlockSpec((1, d), const), pl.BlockSpec((1, d), const)],
        out_specs=out_specs,
        out_shape=out_shape,
        scratch_shapes=[pltpu.VMEM((ROW_TILE, d), F32),
                        pltpu.VMEM((d, dff), BF16),
                        pltpu.VMEM((dff, d), BF16),
                        pltpu.VMEM((2, d, d), F32),
                        pltpu.SemaphoreType.DMA((2,))],
        compiler_params=_params("arbitrary"),
        name="mlp_ln",
    )(x, w1_all, w2_all, g, b)


def _tri_masks(n):
    row = lax.broadcasted_iota(jnp.int32, (n, n), 0)
    col = lax.broadcasted_iota(jnp.int32, (n, n), 1)
    return row >= col, row > col, row == col


def _mlstm_gates(g_raw, gb):
    cap = GATE_CAP * jnp.tanh((g_raw + gb) / GATE_CAP)
    return cap, jax.nn.log_sigmoid(cap)


def _mlstm_gate_block(g_raw, gb, gate_ref, length, valid):
    rows = g_raw.shape[0]
    cap, lf = _mlstm_gates(g_raw, gb)
    pos = lax.broadcasted_iota(jnp.int32, (rows, LANES), 0) % length
    lane_id = lax.broadcasted_iota(jnp.int32, (rows, LANES), 1)
    head_lanes = (lane_id >= H_A) & (lane_id < 2 * H_A)
    i_sh = pltpu.roll(cap, H_A, axis=1)
    if valid < length:
        i_sh = jnp.where(pos < valid, i_sh, NEG_INF)
        lf = jnp.where(pos < valid, lf, 0.0)
    i_sh = jnp.where(head_lanes, i_sh, 0.0)
    r = lax.broadcasted_iota(jnp.int32, (rows, rows), 0)
    c = lax.broadcasted_iota(jnp.int32, (rows, rows), 1)
    tri_bf = ((r >= c) & (r // length == c // length)).astype(BF16)
    bcum = jnp.where(head_lanes, _cumsum_rows(tri_bf, lf), 0.0)
    cmax = i_sh - bcum
    span = 1
    while span < length:
        cmax = jnp.maximum(cmax, jnp.where(pos >= span, pltpu.roll(cmax, span, axis=0), NEG_INF))
        span *= 2
    gate_ref[0] = i_sh
    gate_ref[1] = bcum
    gate_ref[2] = cmax


def _mlstm_chunk(seqs, nw_ref, state_in, state_out, scratch, length):
    c_in, nb_in, m_in = state_in
    c_o, nb_o, m_o = state_out
    q_s, k_s, vx_s, e_s, kwt_s, wi_s, em_s, dc_s = scratch
    nseq = len(seqs)
    groups = nseq * H_A
    causal, _, _ = _tri_masks(length)
    scale = DK_A ** -0.5
    ones = jnp.ones((length, DV_A), BF16)

    for j, (q_ref, k_ref, v_ref, _, gate_ref, rows, _) in enumerate(seqs):
        i_sh = gate_ref[0, rows, :]
        bcum = gate_ref[1, rows, :]
        cmax = gate_ref[2, rows, :]
        cmat = i_sh - bcum
        m_prev = m_in[j]
        m_t = bcum + jnp.maximum(m_prev, cmax)
        a_mat = bcum - m_t
        w_inter = jnp.exp(bcum + m_prev - m_t)
        e_m = jnp.exp(-m_t)
        m_new = m_t[length - 1:length, :]
        b_last = bcum[length - 1:length, :]
        w_k = jnp.exp(b_last - bcum + i_sh - m_new)
        decay = jnp.exp(b_last + m_prev - m_new)
        m_o[j] = m_new
        cmat_t = cmat.T
        for h in range(H_A):
            g = j * H_A + h
            lane = H_A + h
            q_s[g] = q_ref[rows, h * DK_A:(h + 1) * DK_A]
            kh = k_ref[rows, h * DK_A:(h + 1) * DK_A]
            k_s[g] = kh
            vx_s[g, :, 0:DV_A] = v_ref[rows, h * DV_A:(h + 1) * DV_A]
            vx_s[g, :, DV_A:2 * DV_A] = ones
            e_s[g] = jnp.exp(jnp.where(causal, a_mat[:, lane:lane + 1] + cmat_t[lane:lane + 1, :], NEG_INF))
            wi_s[g] = jnp.broadcast_to(w_inter[:, lane:lane + 1], (length, DV_A))
            em_s[g] = jnp.broadcast_to(e_m[:, lane:lane + 1], (length, DV_A))
            kwt_s[g] = (kh.astype(F32) * scale * w_k[:, lane:lane + 1]).T.astype(BF16)
            dc_s[g] = jnp.broadcast_to(decay[:, lane:lane + 1], (1, DV_A))
    qb = q_s[...]
    qk = _bdot_nt(qb, k_s[...])
    c_old = c_in[...].reshape(groups, DK_A, DV_A)
    nb_old = nb_in[...]
    qcx = _bdot(qb, jnp.concatenate([c_old, nb_old], axis=-1).astype(BF16))
    sw = (qk * scale * e_s[...]).astype(BF16)
    vx = vx_s[...]
    svx = _bdot(sw, vx)
    kvx = _bdot(kwt_s[...], vx)
    wi = wi_s[...]
    num = svx[:, :, 0:DV_A] + wi * qcx[:, :, 0:DV_A]
    den = svx[:, :, DV_A:2 * DV_A] + wi * qcx[:, :, DV_A:2 * DV_A]
    hh = num / jnp.maximum(jnp.abs(den), em_s[...])
    dc = dc_s[...]
    c_o[...] = (dc * c_old + kvx[:, :, 0:DV_A]).reshape(c_o.shape)
    nb_o[...] = dc * nb_old + kvx[:, :, DV_A:2 * DV_A]
    hn = _rms(hh) * nw_ref[...]
    for j, (_, _, _, o_ref, _, rows, write_hg) in enumerate(seqs):
        for h in range(H_A):
            cols = slice(h * DV_A, (h + 1) * DV_A)
            gate = _sigmoid(o_ref[rows, cols].astype(F32))
            write_hg(cols, gate * hn[j * H_A + h])


def _n_to_lanes(n_row):
    return jnp.broadcast_to(n_row, (LANES, DK_A)).T


def _n_from_lanes(nb):
    return nb.T[0:1, :]


def _mlstm_scratch(groups, length):
    return [pltpu.VMEM((groups, length, DK_A), BF16),
            pltpu.VMEM((groups, length, DK_A), BF16),
            pltpu.VMEM((groups, length, 2 * DV_A), BF16),
            pltpu.VMEM((groups, length, length), F32),
            pltpu.VMEM((groups, DK_A, length), BF16),
            pltpu.VMEM((groups, length, DV_A), F32),
            pltpu.VMEM((groups, length, DV_A), F32),
            pltpu.VMEM((groups, 1, DV_A), F32)]


def _mlstm_prompt_kernel(*refs, nseq):
    ins = refs[:5 * nseq]
    gb_ref, nw_ref, hg_ref, c_out, n_out, m_out, c_s, n_s, m_s, gate_s = refs[5 * nseq:5 * nseq + 10]
    scratch = refs[5 * nseq + 10:]
    t = pl.program_id(1)

    @pl.when(t == 0)
    def _():
        c_s[...] = jnp.zeros_like(c_s)
        n_s[...] = jnp.zeros_like(n_s)
        m_s[...] = jnp.zeros_like(m_s)

    for j in range(nseq):
        _mlstm_gate_block(ins[4 * nseq + j][...], gb_ref[...], gate_s.at[j], CHUNK, CHUNK)

    def chunk(c, carry):
        rows = pl.ds(pl.multiple_of(c * CHUNK, CHUNK), CHUNK)
        seqs = []
        for j in range(nseq):
            def write_hg(cols, val, j=j):
                hg_ref[j, rows, cols] = val.astype(hg_ref.dtype)
            seqs.append(tuple(ins[i * nseq + j] for i in range(4)) + (gate_s.at[j], rows, write_hg))
        state = (c_s, n_s, m_s)
        _mlstm_chunk(seqs, nw_ref, state, state, scratch, CHUNK)
        return carry

    lax.fori_loop(0, hg_ref.shape[1] // CHUNK, chunk, 0)

    @pl.when(t == pl.num_programs(1) - 1)
    def _():
        c_out[...] = c_s[...].reshape(c_out.shape)
        for j in range(nseq):
            for h in range(H_A):
                n_out[j, h:h + 1, :] = _n_from_lanes(n_s[j * H_A + h])
        m_out[...] = m_s[...]


def _mlstm_prompt(q, k, v, o, g, gb, nw, batch, seq):
    tb = min(MLSTM_ROWS, seq)
    nt = seq // tb
    nseq = MLSTM_SEQS if batch % MLSTM_SEQS == 0 else 1
    groups = nseq * H_A
    const = lambda b, t: (0, 0)

    def rows_of(j):
        return lambda b, t: ((b * nseq + j) * nt + t, 0)

    in_specs, args = [], []
    for arr, width in ((q, QK_A), (k, QK_A), (v, VW_A), (o, VW_A), (g, LANES)):
        for j in range(nseq):
            in_specs.append(pl.BlockSpec((tb, width), rows_of(j)))
            args.append(arr)
    in_specs += [pl.BlockSpec((1, LANES), const), pl.BlockSpec((groups, 1, DV_A), lambda b, t: (0, 0, 0))]
    args += [gb, jnp.tile(nw.reshape(H_A, 1, DV_A), (nseq, 1, 1))]
    hg, c, n, m = pl.pallas_call(
        functools.partial(_mlstm_prompt_kernel, nseq=nseq),
        grid=(batch // nseq, nt),
        in_specs=in_specs,
        out_specs=[pl.BlockSpec((nseq, tb, VW_A), lambda b, t: (b, t, 0)),
                   pl.BlockSpec((nseq, H_A, DK_A, DV_A), lambda b, t: (b, 0, 0, 0)),
                   pl.BlockSpec((nseq, H_A, DK_A), lambda b, t: (b, 0, 0)),
                   pl.BlockSpec((nseq, 1, LANES), lambda b, t: (b, 0, 0))],
        out_shape=[jax.ShapeDtypeStruct((batch, seq, VW_A), BF16),
                   jax.ShapeDtypeStruct((batch, H_A, DK_A, DV_A), F32),
                   jax.ShapeDtypeStruct((batch, H_A, DK_A), F32),
                   jax.ShapeDtypeStruct((batch, 1, LANES), F32)],
        scratch_shapes=[pltpu.VMEM((groups, DK_A, DV_A), F32),
                        pltpu.VMEM((groups, DK_A, LANES), F32),
                        pltpu.VMEM((nseq, 1, LANES), F32),
                        pltpu.VMEM((nseq, 3, tb, LANES), F32)]
        + _mlstm_scratch(groups, CHUNK),
        compiler_params=_params("parallel", "arbitrary"),
        name="mlstm_prompt",
    )(*args)
    return hg.reshape(batch * seq, VW_A), c, n, m


def _mlstm_sample_kernel(q_ref, k_ref, v_ref, o_ref, g_ref, c0_ref, n0_ref, m0_ref, gb_ref, nw_ref, carry_ref,
                         hg_ref, c_out, n_out, m_out, nb_s, gate_s, *scratch, nseq, length, valid):
    del carry_ref
    seqs = []
    for j in range(nseq):
        rows = slice(j * length, (j + 1) * length)
        _mlstm_gate_block(g_ref[rows, :], gb_ref[...], gate_s.at[:, rows, :], length, valid)

        def write_hg(cols, val, rows=rows):
            hg_ref[rows, cols] = val.astype(hg_ref.dtype)
        seqs.append((q_ref, k_ref, v_ref, o_ref, gate_s, rows, write_hg))
        for h in range(H_A):
            nb_s[j * H_A + h] = _n_to_lanes(n0_ref[j, h:h + 1, :])
    _mlstm_chunk(seqs, nw_ref, (c0_ref, nb_s, m0_ref), (c_out, nb_s, m_out), scratch, length)
    for j in range(nseq):
        for h in range(H_A):
            n_out[j, h:h + 1, :] = _n_from_lanes(nb_s[j * H_A + h])


def _mlstm_sample(q, k, v, o, g, c_all, layer, c_carry, n0, m0, gb, nw, length, valid):
    batch = n0.shape[0]
    nseq = MLSTM_SAMPLE_SEQS if batch % MLSTM_SAMPLE_SEQS == 0 else 1
    groups = nseq * H_A
    rows = nseq * length
    row = lambda b: (b, 0)
    b3 = lambda b: (b, 0, 0)
    c_spec = pl.BlockSpec((None, nseq, H_A, DK_A, DV_A), lambda b: (layer, b, 0, 0, 0))
    args = [q, k, v, o, g, c_all, n0, m0, gb, jnp.tile(nw.reshape(H_A, 1, DV_A), (nseq, 1, 1))]
    in_specs = [pl.BlockSpec((rows, QK_A), row), pl.BlockSpec((rows, QK_A), row),
                pl.BlockSpec((rows, VW_A), row), pl.BlockSpec((rows, VW_A), row),
                pl.BlockSpec((rows, LANES), row),
                c_spec,
                pl.BlockSpec((nseq, H_A, DK_A), b3),
                pl.BlockSpec((nseq, 1, LANES), b3),
                pl.BlockSpec((1, LANES), lambda b: (0, 0)),
                pl.BlockSpec((groups, 1, DV_A), lambda b: (0, 0, 0))]
    aliases = {len(args): 1}
    args.append(c_carry)
    in_specs.append(pl.BlockSpec(memory_space=pl.ANY))
    return pl.pallas_call(
        functools.partial(_mlstm_sample_kernel, nseq=nseq, length=length, valid=valid),
        grid=(batch // nseq,),
        in_specs=in_specs,
        out_specs=[pl.BlockSpec((rows, VW_A), row),
                   c_spec,
                   pl.BlockSpec((nseq, H_A, DK_A), b3),
                   pl.BlockSpec((nseq, 1, LANES), b3)],
        out_shape=[jax.ShapeDtypeStruct((batch * length, VW_A), BF16),
                   jax.ShapeDtypeStruct(c_all.shape, F32),
                   jax.ShapeDtypeStruct((batch, H_A, DK_A), F32),
                   jax.ShapeDtypeStruct((batch, 1, LANES), F32)],
        scratch_shapes=[pltpu.VMEM((groups, DK_A, LANES), F32),
                        pltpu.VMEM((3, rows, LANES), F32)]
        + _mlstm_scratch(groups, length),
        input_output_aliases=aliases,
        compiler_params=_params("arbitrary"),
        name="mlstm_sample",
    )(*args)


def _gdn_gates(ba, alog, dtb):
    beta = _sigmoid(ba)
    g = -jnp.exp(alog) * jax.nn.softplus(ba + dtb)
    return beta, g


def _bdot(a, b):
    return jnp.einsum('hmk,hkn->hmn', a, b, preferred_element_type=F32)


def _bdot_nt(a, b):
    return jnp.einsum('hmk,hnk->hmn', a, b, preferred_element_type=F32)


def _unit_lower_solve(nmat, x, order):
    span = 1
    while span < order:
        nh = nmat.astype(BF16)
        xh, xl = _split2(x)
        x = x + (_bdot(nh, xh) + _bdot(nh, xl))
        span *= 2
        if span < order:
            nmat = _bdot(nh, nh)
    return x


CONV_HALO = SUBLANES


def _gdn_chunk(seqs, cw_ref, dtb_ref, alog_ref, nw_ref, s_in, s_o, scratch, length, valid):
    q_s, k_s, bv_s, beg_s, n_s, attn_s, kq_s, kdt_s, gl_s = scratch
    nseq = len(seqs)
    groups = nseq * HV_B
    tril, strict, _ = _tri_masks(length)
    tri_bf = tril.astype(BF16)
    scale = DK_B ** -0.5
    rep = HV_B // HK_B
    live = lax.broadcasted_iota(jnp.int32, (length, LANES), 0) < valid

    gates = []
    for j, (conv_rows, _, ba_ref, rows, _) in enumerate(seqs):
        beta_all, g_all = _gdn_gates(ba_ref[rows, :], alog_ref[...], dtb_ref[...])
        if valid < length:
            beta_all = jnp.where(live, beta_all, 0.0)
            g_all = jnp.where(live, g_all, 0.0)
        gc = _cumsum_rows(tri_bf, g_all)
        g_last = gc[length - 1:length, :]
        gates.append((beta_all, gc, gc.T, jnp.exp(gc), jnp.exp(g_last - gc), jnp.exp(g_last)))
        for kh in range(HK_B):
            q_s[j * HK_B + kh] = _l2norm(_conv_silu(conv_rows, cw_ref, kh * DK_B)) * scale
            k_s[j * HK_B + kh] = _l2norm(_conv_silu(conv_rows, cw_ref, QK_B + kh * DK_B))
    kb = k_s[...].astype(BF16)
    kk = _bdot_nt(kb, kb)
    qk = _bdot_nt(q_s[...].astype(BF16), kb)

    for j, (conv_rows, _, _, _, _) in enumerate(seqs):
        beta_all, gc, gc_t, eg_all, egd_all, gl_all = gates[j]
        for hv in range(HV_B):
            g = j * HV_B + hv
            kh = j * HK_B + hv // rep
            gl = HV_B + hv
            col0 = 2 * QK_B + hv * DV_B
            v = _conv_silu(conv_rows, cw_ref, col0)
            k = k_s[kh]
            beta = beta_all[:, hv:hv + 1]
            eg = eg_all[:, gl:gl + 1]
            decay = jnp.exp(jnp.where(tril, gc[:, gl:gl + 1] - gc_t[gl:gl + 1, :], NEG_INF))
            n_s[g] = jnp.where(strict, -(kk[kh] * beta) * decay, 0.0)
            bv_s[g] = v * beta
            beg_s[g] = jnp.broadcast_to(beta * eg, (length, DV_B))
            attn_s[g] = (qk[kh] * decay).astype(BF16)
            kq_s[g, 0:length, :] = k.astype(BF16)
            kq_s[g, length:2 * length, :] = (q_s[kh] * eg).astype(BF16)
            kdt_s[g] = (k * egd_all[:, gl:gl + 1]).T.astype(BF16)
            gl_s[g] = jnp.broadcast_to(gl_all[:, gl:gl + 1], (1, LANES))

    s = s_in[...].reshape(groups, DK_B, DV_B)
    ks = _bdot(kq_s[...], s.astype(BF16))
    v_new = _unit_lower_solve(n_s[...], bv_s[...] - beg_s[...] * ks[:, 0:length, :], valid)
    vb = v_new.astype(BF16)
    o = ks[:, length:2 * length, :] + _bdot(attn_s[...], vb)
    s_o[...] = (gl_s[...] * s + _bdot(kdt_s[...], vb)).reshape(s_o.shape)
    for j, (_, z_ref, _, rows, write_o) in enumerate(seqs):
        for hv in range(HV_B):
            cols = slice(hv * DV_B, (hv + 1) * DV_B)
            zz = z_ref[rows, cols].astype(F32)
            write_o(cols, _rms(o[j * HV_B + hv]) * nw_ref[:, cols] * (zz * _sigmoid(zz)))


def _conv_silu(conv_rows, cw_ref, col0):
    cw = cw_ref[:, col0:col0 + LANES]
    y = conv_rows(col0)
    length = y.shape[0] - CONV_HALO
    first = CONV_HALO - (CONV_W - 1)
    acc = y[first:first + length] * cw[0:1]
    for j in range(1, CONV_W):
        acc = acc + y[first + j:first + j + length] * cw[j:j + 1]
    return acc * _sigmoid(acc)


def _gdn_scratch(nseq, length):
    groups = nseq * HV_B
    return [pltpu.VMEM((nseq * HK_B, length, DK_B), F32),
            pltpu.VMEM((nseq * HK_B, length, DK_B), F32),
            pltpu.VMEM((groups, length, DV_B), F32),
            pltpu.VMEM((groups, length, DV_B), F32),
            pltpu.VMEM((groups, length, length), F32),
            pltpu.VMEM((groups, length, length), BF16),
            pltpu.VMEM((groups, 2 * length, DK_B), BF16),
            pltpu.VMEM((groups, DK_B, length), BF16),
            pltpu.VMEM((groups, 1, LANES), F32)]


def _gdn_prompt_kernel(qkv_ref, z_ref, ba_ref, cw_ref, dtb_ref, alog_ref, nw_ref,
                       o_ref, s_out, conv_out, xpad_s, s_s, *scratch):
    t = pl.program_id(1)
    tb = qkv_ref.shape[0]

    @pl.when(t == 0)
    def _():
        s_s[...] = jnp.zeros_like(s_s)
        xpad_s[0:CONV_HALO, :] = jnp.zeros((CONV_HALO, CONV_DIM), F32)

    xpad_s[CONV_HALO:CONV_HALO + tb, :] = qkv_ref[...].astype(F32)

    for c in range(tb // CHUNK):
        rows = slice(c * CHUNK, (c + 1) * CHUNK)

        def conv_rows(col0, c=c):
            return xpad_s[c * CHUNK:(c + 1) * CHUNK + CONV_HALO, col0:col0 + LANES]

        def write_o(cols, val, rows=rows):
            o_ref[rows, cols] = val.astype(o_ref.dtype)

        _gdn_chunk([(conv_rows, z_ref, ba_ref, rows, write_o)], cw_ref, dtb_ref, alog_ref, nw_ref,
                   s_s, s_s, scratch, CHUNK, CHUNK)

    xpad_s[0:CONV_HALO, :] = xpad_s[tb:tb + CONV_HALO, :]

    @pl.when(t == pl.num_programs(1) - 1)
    def _():
        s_out[0] = s_s[...]
        conv_out[0] = xpad_s[tb + CONV_HALO - (CONV_W - 1):tb + CONV_HALO, :]


def _gdn_prompt(qkv, z, ba, cw, dtb, alog, nw, batch, seq):
    tb = min(GDN_ROWS, seq)
    nt = seq // tb
    row = lambda b, t: (b * nt + t, 0)
    const = lambda b, t: (0, 0)
    return pl.pallas_call(
        _gdn_prompt_kernel,
        grid=(batch, nt),
        in_specs=[pl.BlockSpec((tb, CONV_DIM), row), pl.BlockSpec((tb, VW_B), row),
                  pl.BlockSpec((tb, LANES), row),
                  pl.BlockSpec((CONV_W, CONV_DIM), const),
                  pl.BlockSpec((1, LANES), const), pl.BlockSpec((1, LANES), const),
                  pl.BlockSpec((1, VW_B), const)],
        out_specs=[pl.BlockSpec((tb, VW_B), row),
                   pl.BlockSpec((1, HV_B, DK_B, DV_B), lambda b, t: (b, 0, 0, 0)),
                   pl.BlockSpec((1, CONV_W - 1, CONV_DIM), lambda b, t: (b, 0, 0))],
        out_shape=[jax.ShapeDtypeStruct((batch * seq, VW_B), BF16),
                   jax.ShapeDtypeStruct((batch, HV_B, DK_B, DV_B), F32),
                   jax.ShapeDtypeStruct((batch, CONV_W - 1, CONV_DIM), F32)],
        scratch_shapes=[pltpu.VMEM((tb + CONV_HALO, CONV_DIM), F32),
                        pltpu.VMEM((HV_B, DK_B, DV_B), F32)]
        + _gdn_scratch(1, CHUNK),
        compiler_params=_params("parallel", "arbitrary"),
        name="gdn_prompt",
    )(qkv, z, ba, cw, dtb, alog, nw)


def _gdn_sample_kernel(xp_ref, z_ref, ba_ref, s0_ref, cw_ref, dtb_ref, alog_ref, nw_ref, carry_ref,
                       o_ref, s_out, *scratch, nseq, length, valid):
    del carry_ref
    seqs = []
    for j in range(nseq):
        rows = slice(j * length, (j + 1) * length)

        def conv_rows(col0, j=j):
            return xp_ref[j, :, col0:col0 + LANES]

        def write_o(cols, val, rows=rows):
            o_ref[rows, cols] = val.astype(o_ref.dtype)
        seqs.append((conv_rows, z_ref, ba_ref, rows, write_o))
    _gdn_chunk(seqs, cw_ref, dtb_ref, alog_ref, nw_ref, s0_ref, s_out, scratch, length, valid)


def _gdn_sample(xp, z, ba, s_all, layer, s_carry, cw, dtb, alog, nw, length, valid):
    batch = xp.shape[0]
    nseq = GDN_SAMPLE_SEQS if batch % GDN_SAMPLE_SEQS == 0 else 1
    rows = nseq * length
    row = lambda b: (b, 0)
    const = lambda b: (0, 0)
    s_spec = pl.BlockSpec((None, nseq, HV_B, DK_B, DV_B), lambda b: (layer, b, 0, 0, 0))
    args = [xp, z, ba, s_all, cw, dtb, alog, nw]
    in_specs = [pl.BlockSpec((nseq,) + xp.shape[1:], lambda b: (b, 0, 0)),
                pl.BlockSpec((rows, VW_B), row),
                pl.BlockSpec((rows, LANES), row),
                s_spec,
                pl.BlockSpec((CONV_W, CONV_DIM), const),
                pl.BlockSpec((1, LANES), const), pl.BlockSpec((1, LANES), const),
                pl.BlockSpec((1, VW_B), const)]
    aliases = {len(args): 1}
    args.append(s_carry)
    in_specs.append(pl.BlockSpec(memory_space=pl.ANY))
    return pl.pallas_call(
        functools.partial(_gdn_sample_kernel, nseq=nseq, length=length, valid=valid),
        grid=(batch // nseq,),
        in_specs=in_specs,
        out_specs=[pl.BlockSpec((rows, VW_B), row), s_spec],
        out_shape=[jax.ShapeDtypeStruct((batch * length, VW_B), BF16),
                   jax.ShapeDtypeStruct(s_all.shape, F32)],
        scratch_shapes=_gdn_scratch(nseq, length),
        input_output_aliases=aliases,
        compiler_params=_params("arbitrary"),
        name="gdn_sample",
    )(*args)


def _pad_cols(w, width):
    return jnp.pad(w, ((0, 0), (0, width - w.shape[1])))


def _lane_row(vec, offset):
    return jnp.zeros((1, LANES), F32).at[0, offset:offset + vec.shape[0]].set(vec.astype(F32))


MLSTM_SEGS = ((0, QK_A), (QK_A, QK_A), (2 * QK_A, VW_A), (2 * QK_A + VW_A, VW_A), (2 * QK_A + 2 * VW_A, LANES))
GDN_SEGS = ((0, CONV_DIM), (CONV_DIM, VW_B), (CONV_DIM + VW_B, LANES))


def _mlstm_layer(x, batch, seq, s_batch, s_seq, w_in, gate_b, norm_w, w_out, c_all, layer, c_carry, n0, m0,
                 ln_g, ln_b):
    w = _pad_cols(w_in, MLSTM_SEGS[-1][0] + LANES).astype(BF16)
    q, k, v, o, g = _proj(x, w, MLSTM_SEGS, (BF16, BF16, BF16, BF16, F32))
    gb = _lane_row(gate_b, 0)
    nw = norm_w.astype(F32).reshape(1, VW_A)
    hg_p, c_p, n_p, m_p = _mlstm_prompt(q, k, v, o, g, gb, nw, batch, seq)

    def padded(a):
        a = a[batch * seq:].reshape(s_batch, s_seq, a.shape[1])
        return jnp.pad(a, ((0, 0), (0, SAMPLE_CHUNK - s_seq), (0, 0))).reshape(s_batch * SAMPLE_CHUNK, a.shape[2])

    m0_lanes = jnp.pad(m0.astype(F32), ((0, 0), (H_A, LANES - 2 * H_A))).reshape(s_batch, 1, LANES)
    hg_s, c_s, n_s, m_s = _mlstm_sample(padded(q), padded(k), padded(v), padded(o), padded(g), c_all, layer,
                                        c_carry, n0, m0_lanes, gb, nw, SAMPLE_CHUNK, s_seq)
    hg_s = hg_s.reshape(s_batch, SAMPLE_CHUNK, VW_A)[:, :s_seq].reshape(s_batch * s_seq, VW_A)
    x = _outproj_ln((hg_p, hg_s), x, w_out.astype(BF16), ln_g.reshape(1, -1), ln_b.reshape(1, -1))
    return x, (c_p, n_p, m_p[:, 0, H_A:2 * H_A]), (c_s, n_s, m_s[:, 0, H_A:2 * H_A])


def _gdn_layer(x, batch, seq, s_batch, s_seq, w_in, conv_w, dt_bias, a_log, norm_w, w_out,
               s_all, layer, s_carry, conv0, ln_g, ln_b):
    w = _pad_cols(w_in, GDN_SEGS[-1][0] + LANES).astype(BF16)
    qkv, z, ba = _proj(x, w, GDN_SEGS, (BF16, BF16, F32))
    cw = conv_w.astype(F32)
    dtb = _lane_row(dt_bias, HV_B)
    alog = _lane_row(a_log, HV_B)
    nw = norm_w.astype(F32).reshape(1, VW_B)
    o_p, s_p, conv_p = _gdn_prompt(qkv, z, ba, cw, dtb, alog, nw, batch, seq)

    def tail(a):
        return a[batch * seq:].reshape(s_batch, s_seq, a.shape[1])

    def padded(a):
        return jnp.pad(tail(a), ((0, 0), (0, SAMPLE_CHUNK - s_seq), (0, 0))).reshape(s_batch * SAMPLE_CHUNK, a.shape[1])

    xp = jnp.concatenate([conv0.astype(F32), tail(qkv).astype(F32)], axis=1)
    conv_s = xp[:, s_seq:]
    front = CONV_HALO - (CONV_W - 1)
    xp = jnp.pad(xp, ((0, 0), (front, SAMPLE_CHUNK - s_seq), (0, 0)))
    o_s, s_s = _gdn_sample(xp, padded(z), padded(ba), s_all, layer, s_carry, cw, dtb, alog, nw,
                           SAMPLE_CHUNK, s_seq)
    o_s = o_s.reshape(s_batch, SAMPLE_CHUNK, VW_B)[:, :s_seq].reshape(s_batch * s_seq, VW_B)
    x = _outproj_ln((o_p, o_s), x, w_out.astype(BF16), ln_g.reshape(1, -1), ln_b.reshape(1, -1))
    return x, (s_p, conv_p), (s_s, conv_s)


def kernel(x_prompt, x_sample, state_mlstm_C, state_mlstm_n, state_mlstm_m, state_gdn_S, state_gdn_conv,
           a_w_in, a_gate_b, a_norm_w, a_w_out, b_w_in, b_conv_w, b_dt_bias, b_a_log, b_norm_w, b_w_out,
           mlp_w1, mlp_w2, ln1_g, ln1_b, ln2_g, ln2_b):
    batch, seq, d = x_prompt.shape
    s_batch, s_seq, _ = x_sample.shape
    n_prompt = batch * seq
    n_sample = s_batch * s_seq
    assert n_prompt % ROW_TILE == 0 and n_sample == ROW_TILE and seq % CHUNK == 0 and s_seq <= SAMPLE_CHUNK
    x = (x_prompt.reshape(n_prompt, d), x_sample.reshape(n_sample, d))
    p_a, s_a, p_b, s_b = [], [], [], []
    c_carry = jnp.zeros(state_mlstm_C.shape, F32)
    s_carry = jnp.zeros(state_gdn_S.shape, F32)
    for layer in range(DEPTH):
        j = layer // 2
        if layer % 2 == 0:
            x, p_state, s_state = _mlstm_layer(
                x, batch, seq, s_batch, s_seq, a_w_in[j], a_gate_b[j], a_norm_w[j], a_w_out[j],
                state_mlstm_C, j, c_carry, state_mlstm_n[j], state_mlstm_m[j], ln1_g[layer], ln1_b[layer])
            c_carry = s_state[0]
            p_a.append(p_state)
            s_a.append(s_state)
        else:
            x, p_state, s_state = _gdn_layer(
                x, batch, seq, s_batch, s_seq, b_w_in[j], b_conv_w[j], b_dt_bias[j], b_a_log[j],
                b_norm_w[j], b_w_out[j], state_gdn_S, j, s_carry, state_gdn_conv[j], ln1_g[layer], ln1_b[layer])
            s_carry = s_state[0]
            p_b.append(p_state)
            s_b.append(s_state)
        x = _mlp_ln(x, mlp_w1, mlp_w2, layer, ln2_g[layer].reshape(1, -1), ln2_b[layer].reshape(1, -1),
                    split_rows=n_prompt if layer == DEPTH - 1 else None)
    y_prompt = x[0].reshape(batch, seq, d)
    y_sample = x[1].reshape(s_batch, s_seq, d)

    def stack(states, i):
        return jnp.stack([s[i] for s in states])

    return (y_prompt, y_sample,
            stack(p_a, 0), stack(p_a, 1), stack(p_a, 2), stack(p_b, 0), stack(p_b, 1),
            c_carry, stack(s_a, 1), stack(s_a, 2), s_carry, stack(s_b, 1))
```

```python
import functools

import jax
import jax.numpy as jnp
from jax import lax
from jax.experimental import pallas as pl
from jax.experimental.pallas import tpu as pltpu

F32 = jnp.float32
BF16 = jnp.bfloat16

D_MODEL = 1024
DEPTH = 4
H_A = 8
DV_A = 128
DK_A = 64
QK_A = H_A * DK_A
VW_A = H_A * DV_A
GATE_CAP = 15.0
HK_B = 8
HV_B = 16
DK_B = 128
DV_B = 128
QK_B = HK_B * DK_B
VW_B = HV_B * DV_B
CONV_W = 4
CONV_DIM = 2 * QK_B + VW_B
CHUNK = 64
D_FF = 4 * D_MODEL
ALPHA = (2.0 * DEPTH) ** 0.25
LN_EPS = 1e-5
RMS_EPS = 1e-6

LANES = 128
SUBLANES = 8
VMEM_LIMIT = 56 * 1024 * 1024
ROW_TILE = 512
FF_TILE = 1024
MLSTM_ROWS = 512
MLSTM_SEQS = 1
MLSTM_SAMPLE_SEQS = 4
GDN_SAMPLE_SEQS = 4
SAMPLE_CHUNK = 16
GDN_SAMPLE_CHUNK = 8
GDN_ROWS = 256

NEG_INF = float("-inf")


def _params(*sem):
    return pltpu.CompilerParams(dimension_semantics=sem, vmem_limit_bytes=VMEM_LIMIT)


def _dot(a, b):
    return jnp.dot(a, b, preferred_element_type=F32)


def _dot_nt(a, b):
    return lax.dot_general(a, b, (((1,), (1,)), ((), ())), preferred_element_type=F32)


def _dot_tn(a, b):
    return lax.dot_general(a, b, (((0,), (0,)), ((), ())), preferred_element_type=F32)


def _split2(x):
    hi = x.astype(BF16)
    lo = (x - hi.astype(F32)).astype(BF16)
    return hi, lo


def _dot_split(a, b):
    ah, al = _split2(a)
    bh, bl = _split2(b)
    return _dot(ah, bh) + (_dot(ah, bl) + _dot(al, bh))


def _cumsum_rows(tri_bf, x):
    x1 = x.astype(BF16)
    r1 = x - x1.astype(F32)
    x2 = r1.astype(BF16)
    x3 = (r1 - x2.astype(F32)).astype(BF16)
    return _dot(tri_bf, x1) + (_dot(tri_bf, x2) + _dot(tri_bf, x3))


def _layer_norm(r, g, b):
    mu = jnp.mean(r, axis=-1, keepdims=True)
    d = r - mu
    var = jnp.mean(d * d, axis=-1, keepdims=True)
    return d * lax.rsqrt(var + LN_EPS) * g + b


def _rms(h):
    return h * lax.rsqrt(jnp.mean(h * h, axis=-1, keepdims=True) + RMS_EPS)


def _l2norm(x):
    return x * lax.rsqrt(jnp.sum(x * x, axis=-1, keepdims=True) + RMS_EPS)


def _sigmoid(x):
    return 0.5 * jnp.tanh(0.5 * x) + 0.5


def _rows(x):
    return x[0].shape[0] + x[1].shape[0] if isinstance(x, tuple) else x.shape[0]


def _row_args(x):
    return list(x) if isinstance(x, tuple) else [x]


def _row_specs(x):
    if isinstance(x, tuple):
        main, tail = x
        last = main.shape[0] // ROW_TILE - 1
        return [pl.BlockSpec((ROW_TILE, main.shape[1]), lambda i: (jnp.minimum(i, last), 0)),
                pl.BlockSpec((ROW_TILE, tail.shape[1]), lambda i: (0, 0))]
    return [pl.BlockSpec((ROW_TILE, x.shape[1]), lambda i: (i, 0))]


def _row_load(refs):
    if len(refs) == 1:
        return refs[0][...]
    on_tail = pl.program_id(0) == pl.num_programs(0) - 1
    return jnp.where(on_tail, refs[1][...], refs[0][...])


def _proj_kernel(*refs, segs, nx):
    xb = _row_load(refs[:nx]).astype(BF16)
    w_ref = refs[nx]
    out_refs = refs[nx + 1:]
    for (start, width), o_ref in zip(segs, out_refs):
        for c0 in range(0, width, FF_TILE):
            cw = min(FF_TILE, width - c0)
            y = _dot(xb, w_ref[:, start + c0:start + c0 + cw])
            o_ref[:, c0:c0 + cw] = y.astype(o_ref.dtype)


def _proj(x, w, segs, dtypes):
    n = _rows(x)
    args = _row_args(x)
    return pl.pallas_call(
        functools.partial(_proj_kernel, segs=segs, nx=len(args)),
        grid=(n // ROW_TILE,),
        in_specs=_row_specs(x) + [pl.BlockSpec(w.shape, lambda i: (0, 0))],
        out_specs=[pl.BlockSpec((ROW_TILE, wd), lambda i: (i, 0)) for _, wd in segs],
        out_shape=[jax.ShapeDtypeStruct((n, wd), dt) for (_, wd), dt in zip(segs, dtypes)],
        compiler_params=_params("arbitrary"),
        name="proj",
    )(*args, w)


def _outproj_ln_kernel(*refs, nh, nx):
    w_ref, g_ref, b_ref, o_ref = refs[nh + nx:]
    y = _dot(_row_load(refs[:nh]), w_ref[...])
    o_ref[...] = _layer_norm(ALPHA * _row_load(refs[nh:nh + nx]) + y, g_ref[...], b_ref[...])


def _outproj_ln(h, x, w, g, b):
    kh, d = w.shape
    n = _rows(x)
    const = lambda i: (0, 0)
    h_args, x_args = _row_args(h), _row_args(x)
    return pl.pallas_call(
        functools.partial(_outproj_ln_kernel, nh=len(h_args), nx=len(x_args)),
        grid=(n // ROW_TILE,),
        in_specs=_row_specs(h) + _row_specs(x) + [pl.BlockSpec((kh, d), const),
                                                  pl.BlockSpec((1, d), const), pl.BlockSpec((1, d), const)],
        out_specs=pl.BlockSpec((ROW_TILE, d), lambda i: (i, 0)),
        out_shape=jax.ShapeDtypeStruct((n, d), F32),
        compiler_params=_params("arbitrary"),
        name="outproj_ln",
    )(*h_args, *x_args, w, g, b)


def _stage_weights_bf16(w1_hbm, w2_hbm, layer, w1_ref, w2_ref, stage, sem):
    d, dff = w1_ref.shape
    step = stage.shape[1]
    assert stage.shape == (2, step, step) and d == step and dff % step == 0
    plan = ([(w1_hbm.at[layer, :, c:c + step], w1_ref.at[:, c:c + step]) for c in range(0, dff, step)]
            + [(w2_hbm.at[layer, c:c + step, :], w2_ref.at[c:c + step, :]) for c in range(0, dff, step)])
    copies = [pltpu.make_async_copy(src, stage.at[i % 2], sem.at[i % 2]) for i, (src, _) in enumerate(plan)]
    copies[0].start()
    for i, (_, dst) in enumerate(plan):
        if i + 1 < len(plan):
            copies[i + 1].start()
        copies[i].wait()
        dst[...] = stage[i % 2].astype(BF16)


def _mlp_ln_kernel(x_ref, w1_hbm, w2_hbm, g_ref, b_ref, *out_and_scratch, split, layer):
    acc_ref, w1_ref, w2_ref, stage, sem = out_and_scratch[-5:]

    @pl.when(pl.program_id(0) == 0)
    def _():
        _stage_weights_bf16(w1_hbm, w2_hbm, layer, w1_ref, w2_ref, stage, sem)

    x = x_ref[...]
    xb = x.astype(BF16)
    dff = w1_ref.shape[1]
    for c0 in range(0, dff, FF_TILE):
        h = _dot(xb, w1_ref[:, c0:c0 + FF_TILE])
        h = jnp.square(jnp.maximum(h, 0.0)).astype(BF16)
        y = _dot(h, w2_ref[c0:c0 + FF_TILE, :])
        if c0 == 0:
            acc_ref[...] = y
        else:
            acc_ref[...] += y
    res = _layer_norm(ALPHA * x + acc_ref[...], g_ref[...], b_ref[...])
    if not split:
        out_and_scratch[0][...] = res
        return
    main_ref, tail_ref = out_and_scratch[:2]
    on_tail = pl.program_id(0) == pl.num_programs(0) - 1

    @pl.when(jnp.logical_not(on_tail))
    def _():
        main_ref[...] = res

    @pl.when(on_tail)
    def _():
        tail_ref[...] = res


def _mlp_ln(x, w1_all, w2_all, layer, g, b, split_rows=None):
    n = x.shape[0]
    _, d, dff = w1_all.shape
    const = lambda i: (0, 0)
    if split_rows is None:
        out_specs = pl.BlockSpec((ROW_TILE, d), lambda i: (i, 0))
        out_shape = jax.ShapeDtypeStruct((n, d), F32)
    else:
        assert n - split_rows == ROW_TILE
        last = split_rows // ROW_TILE - 1
        out_specs = [pl.BlockSpec((ROW_TILE, d), lambda i: (jnp.minimum(i, last), 0)),
                     pl.BlockSpec((ROW_TILE, d), lambda i: (0, 0))]
        out_shape = [jax.ShapeDtypeStruct((split_rows, d), F32), jax.ShapeDtypeStruct((ROW_TILE, d), F32)]
    return pl.pallas_call(
        functools.partial(_mlp_ln_kernel, split=split_rows is not None, layer=layer),
        grid=(n // ROW_TILE,),
        in_specs=[pl.BlockSpec((ROW_TILE, d), lambda i: (i, 0)),
                  pl.BlockSpec(memory_space=pl.ANY), pl.BlockSpec(memory_space=pl.ANY),
                  pl.BlockSpec((1, d), const), pl.BlockSpec((1, d), const)],
        out_specs=out_specs,
        out_shape=out_shape,
        scratch_shapes=[pltpu.VMEM((ROW_TILE, d), F32),
                        pltpu.VMEM((d, dff), BF16),
                        pltpu.VMEM((dff, d), BF16),
                        pltpu.VMEM((2, d, d), F32),
                        pltpu.SemaphoreType.DMA((2,))],
        compiler_params=_params("arbitrary"),
        name="mlp_ln",
    )(x, w1_all, w2_all, g, b)


def _tri_masks(n):
    row = lax.broadcasted_iota(jnp.int32, (n, n), 0)
    col = lax.broadcasted_iota(jnp.int32, (n, n), 1)
    return row >= col, row > col, row == col


def _mlstm_gates(g_raw, gb):
    cap = GATE_CAP * jnp.tanh((g_raw + gb) / GATE_CAP)
    return cap, jax.nn.log_sigmoid(cap)


def _mlstm_gate_block(g_raw, gb, gate_ref, length, valid):
    rows = g_raw.shape[0]
    cap, lf = _mlstm_gates(g_raw, gb)
    pos = lax.broadcasted_iota(jnp.int32, (rows, LANES), 0) % length
    lane_id = lax.broadcasted_iota(jnp.int32, (rows, LANES), 1)
    head_lanes = (lane_id >= H_A) & (lane_id < 2 * H_A)
    i_sh = pltpu.roll(cap, H_A, axis=1)
    if valid < length:
        i_sh = jnp.where(pos < valid, i_sh, NEG_INF)
        lf = jnp.where(pos < valid, lf, 0.0)
    i_sh = jnp.where(head_lanes, i_sh, 0.0)
    r = lax.broadcasted_iota(jnp.int32, (rows, rows), 0)
    c = lax.broadcasted_iota(jnp.int32, (rows, rows), 1)
    tri_bf = ((r >= c) & (r // length == c // length)).astype(BF16)
    bcum = jnp.where(head_lanes, _cumsum_rows(tri_bf, lf), 0.0)
    cmax = i_sh - bcum
    span = 1
    while span < length:
        cmax = jnp.maximum(cmax, jnp.where(pos >= span, pltpu.roll(cmax, span, axis=0), NEG_INF))
        span *= 2
    gate_ref[0] = i_sh
    gate_ref[1] = bcum
    gate_ref[2] = cmax


def _mlstm_chunk(seqs, nw_ref, state_in, state_out, scratch, length):
    c_in, nb_in, m_in = state_in
    c_o, nb_o, m_o = state_out
    q_s, k_s, vx_s, e_s, kwt_s, wi_s, em_s, dc_s = scratch
    nseq = len(seqs)
    groups = nseq * H_A
    causal, _, _ = _tri_masks(length)
    scale = DK_A ** -0.5
    ones = jnp.ones((length, DV_A), BF16)

    for j, (q_ref, k_ref, v_ref, _, gate_ref, rows, _) in enumerate(seqs):
        i_sh = gate_ref[0, rows, :]
        bcum = gate_ref[1, rows, :]
        cmax = gate_ref[2, rows, :]
        cmat = i_sh - bcum
        m_prev = m_in[j]
        m_t = bcum + jnp.maximum(m_prev, cmax)
        a_mat = bcum - m_t
        w_inter = jnp.exp(bcum + m_prev - m_t)
        e_m = jnp.exp(-m_t)
        m_new = m_t[length - 1:length, :]
        b_last = bcum[length - 1:length, :]
        w_k = jnp.exp(b_last - bcum + i_sh - m_new)
        decay = jnp.exp(b_last + m_prev - m_new)
        m_o[j] = m_new
        cmat_t = cmat.T
        for h in range(H_A):
            g = j * H_A + h
            lane = H_A + h
            q_s[g] = q_ref[rows, h * DK_A:(h + 1) * DK_A]
            kh = k_ref[rows, h * DK_A:(h + 1) * DK_A]
            k_s[g] = kh
            vx_s[g, :, 0:DV_A] = v_ref[rows, h * DV_A:(h + 1) * DV_A]
            vx_s[g, :, DV_A:2 * DV_A] = ones
            e_s[g] = jnp.exp(jnp.where(causal, a_mat[:, lane:lane + 1] + cmat_t[lane:lane + 1, :], NEG_INF))
            wi_s[g] = jnp.broadcast_to(w_inter[:, lane:lane + 1], (length, DV_A))
            em_s[g] = jnp.broadcast_to(e_m[:, lane:lane + 1], (length, DV_A))
            kwt_s[g] = (kh.astype(F32) * scale * w_k[:, lane:lane + 1]).T.astype(BF16)
            dc_s[g] = jnp.broadcast_to(decay[:, lane:lane + 1], (1, DV_A))
    qb = q_s[...]
    qk = _bdot_nt(qb, k_s[...])
    c_old = c_in[...].reshape(groups, DK_A, DV_A)
    nb_old = nb_in[...]
    qcx = _bdot(qb, jnp.concatenate([c_old, nb_old], axis=-1).astype(BF16))
    sw = (qk * scale * e_s[...]).astype(BF16)
    vx = vx_s[...]
    svx = _bdot(sw, vx)
    kvx = _bdot(kwt_s[...], vx)
    wi = wi_s[...]
    num = svx[:, :, 0:DV_A] + wi * qcx[:, :, 0:DV_A]
    den = svx[:, :, DV_A:2 * DV_A] + wi * qcx[:, :, DV_A:2 * DV_A]
    hh = num / jnp.maximum(jnp.abs(den), em_s[...])
    dc = dc_s[...]
    c_o[...] = (dc * c_old + kvx[:, :, 0:DV_A]).reshape(c_o.shape)
    nb_o[...] = dc * nb_old + kvx[:, :, DV_A:2 * DV_A]
    hn = _rms(hh) * nw_ref[...]
    for j, (_, _, _, o_ref, _, rows, write_hg) in enumerate(seqs):
        for h in range(H_A):
            cols = slice(h * DV_A, (h + 1) * DV_A)
            gate = _sigmoid(o_ref[rows, cols].astype(F32))
            write_hg(cols, gate * hn[j * H_A + h])


def _n_to_lanes(n_row):
    return jnp.broadcast_to(n_row, (LANES, DK_A)).T


def _n_from_lanes(nb):
    return nb.T[0:1, :]


def _mlstm_scratch(groups, length):
    return [pltpu.VMEM((groups, length, DK_A), BF16),
            pltpu.VMEM((groups, length, DK_A), BF16),
            pltpu.VMEM((groups, length, 2 * DV_A), BF16),
            pltpu.VMEM((groups, length, length), F32),
            pltpu.VMEM((groups, DK_A, length), BF16),
            pltpu.VMEM((groups, length, DV_A), F32),
            pltpu.VMEM((groups, length, DV_A), F32),
            pltpu.VMEM((groups, 1, DV_A), F32)]


def _mlstm_prompt_kernel(*refs, nseq):
    ins = refs[:5 * nseq]
    gb_ref, nw_ref, hg_ref, c_out, n_out, m_out, c_s, n_s, m_s, gate_s = refs[5 * nseq:5 * nseq + 10]
    scratch = refs[5 * nseq + 10:]
    t = pl.program_id(1)

    @pl.when(t == 0)
    def _():
        c_s[...] = jnp.zeros_like(c_s)
        n_s[...] = jnp.zeros_like(n_s)
        m_s[...] = jnp.zeros_like(m_s)

    for j in range(nseq):
        _mlstm_gate_block(ins[4 * nseq + j][...], gb_ref[...], gate_s.at[j], CHUNK, CHUNK)

    def chunk(c, carry):
        rows = pl.ds(pl.multiple_of(c * CHUNK, CHUNK), CHUNK)
        seqs = []
        for j in range(nseq):
            def write_hg(cols, val, j=j):
                hg_ref[j, rows, cols] = val.astype(hg_ref.dtype)
            seqs.append(tuple(ins[i * nseq + j] for i in range(4)) + (gate_s.at[j], rows, write_hg))
        state = (c_s, n_s, m_s)
        _mlstm_chunk(seqs, nw_ref, state, state, scratch, CHUNK)
        return carry

    lax.fori_loop(0, hg_ref.shape[1] // CHUNK, chunk, 0)

    @pl.when(t == pl.num_programs(1) - 1)
    def _():
        c_out[...] = c_s[...].reshape(c_out.shape)
        for j in range(nseq):
            for h in range(H_A):
                n_out[j, h:h + 1, :] = _n_from_lanes(n_s[j * H_A + h])
        m_out[...] = m_s[...]


def _mlstm_prompt(q, k, v, o, g, gb, nw, batch, seq):
    tb = min(MLSTM_ROWS, seq)
    nt = seq // tb
    nseq = MLSTM_SEQS if batch % MLSTM_SEQS == 0 else 1
    groups = nseq * H_A
    const = lambda b, t: (0, 0)

    def rows_of(j):
        return lambda b, t: ((b * nseq + j) * nt + t, 0)

    in_specs, args = [], []
    for arr, width in ((q, QK_A), (k, QK_A), (v, VW_A), (o, VW_A), (g, LANES)):
        for j in range(nseq):
            in_specs.append(pl.BlockSpec((tb, width), rows_of(j)))
            args.append(arr)
    in_specs += [pl.BlockSpec((1, LANES), const), pl.BlockSpec((groups, 1, DV_A), lambda b, t: (0, 0, 0))]
    args += [gb, jnp.tile(nw.reshape(H_A, 1, DV_A), (nseq, 1, 1))]
    hg, c, n, m = pl.pallas_call(
        functools.partial(_mlstm_prompt_kernel, nseq=nseq),
        grid=(batch // nseq, nt),
        in_specs=in_specs,
        out_specs=[pl.BlockSpec((nseq, tb, VW_A), lambda b, t: (b, t, 0)),
                   pl.BlockSpec((nseq, H_A, DK_A, DV_A), lambda b, t: (b, 0, 0, 0)),
                   pl.BlockSpec((nseq, H_A, DK_A), lambda b, t: (b, 0, 0)),
                   pl.BlockSpec((nseq, 1, LANES), lambda b, t: (b, 0, 0))],
        out_shape=[jax.ShapeDtypeStruct((batch, seq, VW_A), BF16),
                   jax.ShapeDtypeStruct((batch, H_A, DK_A, DV_A), F32),
                   jax.ShapeDtypeStruct((batch, H_A, DK_A), F32),
                   jax.ShapeDtypeStruct((batch, 1, LANES), F32)],
        scratch_shapes=[pltpu.VMEM((groups, DK_A, DV_A), F32),
                        pltpu.VMEM((groups, DK_A, LANES), F32),
                        pltpu.VMEM((nseq, 1, LANES), F32),
                        pltpu.VMEM((nseq, 3, tb, LANES), F32)]
        + _mlstm_scratch(groups, CHUNK),
        compiler_params=_params("parallel", "arbitrary"),
        name="mlstm_prompt",
    )(*args)
    return hg.reshape(batch * seq, VW_A), c, n, m


def _mlstm_sample_kernel(q_ref, k_ref, v_ref, o_ref, g_ref, c0_ref, n0_ref, m0_ref, gb_ref, nw_ref, carry_ref,
                         hg_ref, c_out, n_out, m_out, nb_s, gate_s, *scratch, nseq, length, valid):
    del carry_ref
    seqs = []
    for j in range(nseq):
        rows = slice(j * length, (j + 1) * length)
        _mlstm_gate_block(g_ref[rows, :], gb_ref[...], gate_s.at[:, rows, :], length, valid)

        def write_hg(cols, val, rows=rows):
            hg_ref[rows, cols] = val.astype(hg_ref.dtype)
        seqs.append((q_ref, k_ref, v_ref, o_ref, gate_s, rows, write_hg))
        for h in range(H_A):
            nb_s[j * H_A + h] = _n_to_lanes(n0_ref[j, h:h + 1, :])
    _mlstm_chunk(seqs, nw_ref, (c0_ref, nb_s, m0_ref), (c_out, nb_s, m_out), scratch, length)
    for j in range(nseq):
        for h in range(H_A):
            n_out[j, h:h + 1, :] = _n_from_lanes(nb_s[j * H_A + h])


def _mlstm_sample(q, k, v, o, g, c_all, layer, c_carry, n0, m0, gb, nw, length, valid):
    batch = n0.shape[0]
    nseq = MLSTM_SAMPLE_SEQS if batch % MLSTM_SAMPLE_SEQS == 0 else 1
    groups = nseq * H_A
    rows = nseq * length
    row = lambda b: (b, 0)
    b3 = lambda b: (b, 0, 0)
    c_spec = pl.BlockSpec((None, nseq, H_A, DK_A, DV_A), lambda b: (layer, b, 0, 0, 0))
    args = [q, k, v, o, g, c_all, n0, m0, gb, jnp.tile(nw.reshape(H_A, 1, DV_A), (nseq, 1, 1))]
    in_specs = [pl.BlockSpec((rows, QK_A), row), pl.BlockSpec((rows, QK_A), row),
                pl.BlockSpec((rows, VW_A), row), pl.BlockSpec((rows, VW_A), row),
                pl.BlockSpec((rows, LANES), row),
                c_spec,
                pl.BlockSpec((nseq, H_A, DK_A), b3),
                pl.BlockSpec((nseq, 1, LANES), b3),
                pl.BlockSpec((1, LANES), lambda b: (0, 0)),
                pl.BlockSpec((groups, 1, DV_A), lambda b: (0, 0, 0))]
    aliases = {len(args): 1}
    args.append(c_carry)
    in_specs.append(pl.BlockSpec(memory_space=pl.ANY))
    return pl.pallas_call(
        functools.partial(_mlstm_sample_kernel, nseq=nseq, length=length, valid=valid),
        grid=(batch // nseq,),
        in_specs=in_specs,
        out_specs=[pl.BlockSpec((rows, VW_A), row),
                   c_spec,
                   pl.BlockSpec((nseq, H_A, DK_A), b3),
                   pl.BlockSpec((nseq, 1, LANES), b3)],
        out_shape=[jax.ShapeDtypeStruct((batch * length, VW_A), BF16),
                   jax.ShapeDtypeStruct(c_all.shape, F32),
                   jax.ShapeDtypeStruct((batch, H_A, DK_A), F32),
                   jax.ShapeDtypeStruct((batch, 1, LANES), F32)],
        scratch_shapes=[pltpu.VMEM((groups, DK_A, LANES), F32),
                        pltpu.VMEM((3, rows, LANES), F32)]
        + _mlstm_scratch(groups, length),
        input_output_aliases=aliases,
        compiler_params=_params("arbitrary"),
        name="mlstm_sample",
    )(*args)


def _gdn_gates(ba, alog, dtb):
    beta = _sigmoid(ba)
    g = -jnp.exp(alog) * jax.nn.softplus(ba + dtb)
    return beta, g


def _bdot(a, b):
    return jnp.einsum('hmk,hkn->hmn', a, b, preferred_element_type=F32)


def _bdot_nt(a, b):
    return jnp.einsum('hmk,hnk->hmn', a, b, preferred_element_type=F32)


def _unit_lower_solve(nmat, x, order):
    span = 1
    while span < order:
        nh = nmat.astype(BF16)
        xh, xl = _split2(x)
        x = x + (_bdot(nh, xh) + _bdot(nh, xl))
        span *= 2
        if span < order:
            nmat = _bdot(nh, nh)
    return x


CONV_HALO = SUBLANES


def _gdn_chunk(seqs, cw_ref, dtb_ref, alog_ref, nw_ref, s_in, s_o, scratch, length, valid):
    q_s, k_s, bv_s, beg_s, n_s, attn_s, kq_s, kdt_s, gl_s = scratch
    nseq = len(seqs)
    groups = nseq * HV_B
    tril, strict, _ = _tri_masks(length)
    tri_bf = tril.astype(BF16)
    scale = DK_B ** -0.5
    rep = HV_B // HK_B
    live = lax.broadcasted_iota(jnp.int32, (length, LANES), 0) < valid

    gates = []
    for j, (conv_rows, _, ba_ref, rows, _) in enumerate(seqs):
        beta_all, g_all = _gdn_gates(ba_ref[rows, :], alog_ref[...], dtb_ref[...])
        if valid < length:
            beta_all = jnp.where(live, beta_all, 0.0)
            g_all = jnp.where(live, g_all, 0.0)
        gc = _cumsum_rows(tri_bf, g_all)
        g_last = gc[length - 1:length, :]
        gates.append((beta_all, gc, gc.T, jnp.exp(gc), jnp.exp(g_last - gc), jnp.exp(g_last)))
        for kh in range(HK_B):
            q_s[j * HK_B + kh] = _l2norm(_conv_silu(conv_rows, cw_ref, kh * DK_B)) * scale
            k_s[j * HK_B + kh] = _l2norm(_conv_silu(conv_rows, cw_ref, QK_B + kh * DK_B))
    kb = k_s[...].astype(BF16)
    kk = _bdot_nt(kb, kb)
    qk = _bdot_nt(q_s[...].astype(BF16), kb)

    for j, (conv_rows, _, _, _, _) in enumerate(seqs):
        beta_all, gc, gc_t, eg_all, egd_all, gl_all = gates[j]
        for hv in range(HV_B):
            g = j * HV_B + hv
            kh = j * HK_B + hv // rep
            gl = HV_B + hv
            col0 = 2 * QK_B + hv * DV_B
            v = _conv_silu(conv_rows, cw_ref, col0)
            k = k_s[kh]
            beta = beta_all[:, hv:hv + 1]
            eg = eg_all[:, gl:gl + 1]
            decay = jnp.exp(jnp.where(tril, gc[:, gl:gl + 1] - gc_t[gl:gl + 1, :], NEG_INF))
            n_s[g] = jnp.where(strict, -(kk[kh] * beta) * decay, 0.0)
            bv_s[g] = v * beta
            beg_s[g] = jnp.broadcast_to(beta * eg, (length, DV_B))
            attn_s[g] = (qk[kh] * decay).astype(BF16)
            kq_s[g, 0:length, :] = k.astype(BF16)
            kq_s[g, length:2 * length, :] = (q_s[kh] * eg).astype(BF16)
            kdt_s[g] = (k * egd_all[:, gl:gl + 1]).T.astype(BF16)
            gl_s[g] = jnp.broadcast_to(gl_all[:, gl:gl + 1], (1, LANES))

    s = s_in[...].reshape(groups, DK_B, DV_B)
    ks = _bdot(kq_s[...], s.astype(BF16))
    v_new = _unit_lower_solve(n_s[...], bv_s[...] - beg_s[...] * ks[:, 0:length, :], valid)
    vb = v_new.astype(BF16)
    o = ks[:, length:2 * length, :] + _bdot(attn_s[...], vb)
    s_o[...] = (gl_s[...] * s + _bdot(kdt_s[...], vb)).reshape(s_o.shape)
    for j, (_, z_ref, _, rows, write_o) in enumerate(seqs):
        for hv in range(HV_B):
            cols = slice(hv * DV_B, (hv + 1) * DV_B)
            zz = z_ref[rows, cols].astype(F32)
            write_o(cols, _rms(o[j * HV_B + hv]) * nw_ref[:, cols] * (zz * _sigmoid(zz)))


def _conv_silu(conv_rows, cw_ref, col0):
    cw = cw_ref[:, col0:col0 + LANES]
    y = conv_rows(col0)
    length = y.shape[0] - CONV_HALO
    first = CONV_HALO - (CONV_W - 1)
    acc = y[first:first + length] * cw[0:1]
    for j in range(1, CONV_W):
        acc = acc + y[first + j:first + j + length] * cw[j:j + 1]
    return acc * _sigmoid(acc)


def _gdn_scratch(nseq, length):
    groups = nseq * HV_B
    return [pltpu.VMEM((nseq * HK_B, length, DK_B), F32),
            pltpu.VMEM((nseq * HK_B, length, DK_B), F32),
            pltpu.VMEM((groups, length, DV_B), F32),
            pltpu.VMEM((groups, length, DV_B), F32),
            pltpu.VMEM((groups, length, length), F32),
            pltpu.VMEM((groups, length, length), BF16),
            pltpu.VMEM((groups, 2 * length, DK_B), BF16),
            pltpu.VMEM((groups, DK_B, length), BF16),
            pltpu.VMEM((groups, 1, LANES), F32)]


def _gdn_prompt_kernel(qkv_ref, z_ref, ba_ref, cw_ref, dtb_ref, alog_ref, nw_ref,
                       o_ref, s_out, conv_out, xpad_s, s_s, *scratch):
    t = pl.program_id(1)
    tb = qkv_ref.shape[0]

    @pl.when(t == 0)
    def _():
        s_s[...] = jnp.zeros_like(s_s)
        xpad_s[0:CONV_HALO, :] = jnp.zeros((CONV_HALO, CONV_DIM), F32)

    xpad_s[CONV_HALO:CONV_HALO + tb, :] = qkv_ref[...].astype(F32)

    for c in range(tb // CHUNK):
        rows = slice(c * CHUNK, (c + 1) * CHUNK)

        def conv_rows(col0, c=c):
            return xpad_s[c * CHUNK:(c + 1) * CHUNK + CONV_HALO, col0:col0 + LANES]

        def write_o(cols, val, rows=rows):
            o_ref[rows, cols] = val.astype(o_ref.dtype)

        _gdn_chunk([(conv_rows, z_ref, ba_ref, rows, write_o)], cw_ref, dtb_ref, alog_ref, nw_ref,
                   s_s, s_s, scratch, CHUNK, CHUNK)

    xpad_s[0:CONV_HALO, :] = xpad_s[tb:tb + CONV_HALO, :]

    @pl.when(t == pl.num_programs(1) - 1)
    def _():
        s_out[0] = s_s[...]
        conv_out[0] = xpad_s[tb + CONV_HALO - (CONV_W - 1):tb + CONV_HALO, :]


def _gdn_prompt(qkv, z, ba, cw, dtb, alog, nw, batch, seq):
    tb = min(GDN_ROWS, seq)
    nt = seq // tb
    row = lambda b, t: (b * nt + t, 0)
    const = lambda b, t: (0, 0)
    return pl.pallas_call(
        _gdn_prompt_kernel,
        grid=(batch, nt),
        in_specs=[pl.BlockSpec((tb, CONV_DIM), row), pl.BlockSpec((tb, VW_B), row),
                  pl.BlockSpec((tb, LANES), row),
                  pl.BlockSpec((CONV_W, CONV_DIM), const),
                  pl.BlockSpec((1, LANES), const), pl.BlockSpec((1, LANES), const),
                  pl.BlockSpec((1, VW_B), const)],
        out_specs=[pl.BlockSpec((tb, VW_B), row),
                   pl.BlockSpec((1, HV_B, DK_B, DV_B), lambda b, t: (b, 0, 0, 0)),
                   pl.BlockSpec((1, CONV_W - 1, CONV_DIM), lambda b, t: (b, 0, 0))],
        out_shape=[jax.ShapeDtypeStruct((batch * seq, VW_B), BF16),
                   jax.ShapeDtypeStruct((batch, HV_B, DK_B, DV_B), F32),
                   jax.ShapeDtypeStruct((batch, CONV_W - 1, CONV_DIM), F32)],
        scratch_shapes=[pltpu.VMEM((tb + CONV_HALO, CONV_DIM), F32),
                        pltpu.VMEM((HV_B, DK_B, DV_B), F32)]
        + _gdn_scratch(1, CHUNK),
        compiler_params=_params("parallel", "arbitrary"),
        name="gdn_prompt",
    )(qkv, z, ba, cw, dtb, alog, nw)


def _gdn_sample_kernel(xp_ref, z_ref, ba_ref, s0_ref, cw_ref, dtb_ref, alog_ref, nw_ref, carry_ref,
                       o_ref, s_out, *scratch, nseq, length, valid):
    del carry_ref
    seqs = []
    for j in range(nseq):
        rows = slice(j * length, (j + 1) * length)

        def conv_rows(col0, j=j):
            return xp_ref[j, :, col0:col0 + LANES]

        def write_o(cols, val, rows=rows):
            o_ref[rows, cols] = val.astype(o_ref.dtype)
        seqs.append((conv_rows, z_ref, ba_ref, rows, write_o))
    _gdn_chunk(seqs, cw_ref, dtb_ref, alog_ref, nw_ref, s0_ref, s_out, scratch, length, valid)


def _gdn_sample(xp, z, ba, s_all, layer, s_carry, cw, dtb, alog, nw, length, valid):
    batch = xp.shape[0]
    nseq = GDN_SAMPLE_SEQS if batch % GDN_SAMPLE_SEQS == 0 else 1
    rows = nseq * length
    row = lambda b: (b, 0)
    const = lambda b: (0, 0)
    s_spec = pl.BlockSpec((None, nseq, HV_B, DK_B, DV_B), lambda b: (layer, b, 0, 0, 0))
    args = [xp, z, ba, s_all, cw, dtb, alog, nw]
    in_specs = [pl.BlockSpec((nseq,) + xp.shape[1:], lambda b: (b, 0, 0)),
                pl.BlockSpec((rows, VW_B), row),
                pl.BlockSpec((rows, LANES), row),
                s_spec,
                pl.BlockSpec((CONV_W, CONV_DIM), const),
                pl.BlockSpec((1, LANES), const), pl.BlockSpec((1, LANES), const),
                pl.BlockSpec((1, VW_B), const)]
    aliases = {len(args): 1}
    args.append(s_carry)
    in_specs.append(pl.BlockSpec(memory_space=pl.ANY))
    return pl.pallas_call(
        functools.partial(_gdn_sample_kernel, nseq=nseq, length=length, valid=valid),
        grid=(batch // nseq,),
        in_specs=in_specs,
        out_specs=[pl.BlockSpec((rows, VW_B), row), s_spec],
        out_shape=[jax.ShapeDtypeStruct((batch * length, VW_B), F32),
                   jax.ShapeDtypeStruct(s_all.shape, F32)],
        scratch_shapes=_gdn_scratch(nseq, length),
        input_output_aliases=aliases,
        compiler_params=_params("arbitrary"),
        name="gdn_sample",
    )(*args)


def _pad_cols(w, width):
    return jnp.pad(w, ((0, 0), (0, width - w.shape[1])))


def _lane_row(vec, offset):
    return jnp.zeros((1, LANES), F32).at[0, offset:offset + vec.shape[0]].set(vec.astype(F32))


MLSTM_SEGS = ((0, QK_A), (QK_A, QK_A), (2 * QK_A, VW_A), (2 * QK_A + VW_A, VW_A), (2 * QK_A + 2 * VW_A, LANES))
GDN_SEGS = ((0, CONV_DIM), (CONV_DIM, VW_B), (CONV_DIM + VW_B, LANES))


def _mlstm_layer(x, batch, seq, s_batch, s_seq, w_in, gate_b, norm_w, w_out, c_all, layer, c_carry, n0, m0,
                 ln_g, ln_b):
    w = _pad_cols(w_in, MLSTM_SEGS[-1][0] + LANES).astype(BF16)
    q, k, v, o, g = _proj(x, w, MLSTM_SEGS, (BF16, BF16, BF16, BF16, F32))
    gb = _lane_row(gate_b, 0)
    nw = norm_w.astype(F32).reshape(1, VW_A)
    hg_p, c_p, n_p, m_p = _mlstm_prompt(q, k, v, o, g, gb, nw, batch, seq)

    def padded(a):
        a = a[batch * seq:].reshape(s_batch, s_seq, a.shape[1])
        return jnp.pad(a, ((0, 0), (0, SAMPLE_CHUNK - s_seq), (0, 0))).reshape(s_batch * SAMPLE_CHUNK, a.shape[2])

    m0_lanes = jnp.pad(m0.astype(F32), ((0, 0), (H_A, LANES - 2 * H_A))).reshape(s_batch, 1, LANES)
    hg_s, c_s, n_s, m_s = _mlstm_sample(padded(q), padded(k), padded(v), padded(o), padded(g), c_all, layer,
                                        c_carry, n0, m0_lanes, gb, nw, SAMPLE_CHUNK, s_seq)
    hg_s = hg_s.reshape(s_batch, SAMPLE_CHUNK, VW_A)[:, :s_seq].reshape(s_batch * s_seq, VW_A)
    x = _outproj_ln((hg_p, hg_s), x, w_out.astype(BF16), ln_g.reshape(1, -1), ln_b.reshape(1, -1))
    return x, (c_p, n_p, m_p[:, 0, H_A:2 * H_A]), (c_s, n_s, m_s[:, 0, H_A:2 * H_A])


def _gdn_layer(x, batch, seq, s_batch, s_seq, w_in, conv_w, dt_bias, a_log, norm_w, w_out,
               s_all, layer, s_carry, conv0, ln_g, ln_b):
    w = _pad_cols(w_in, GDN_SEGS[-1][0] + LANES).astype(BF16)
    qkv, z, ba = _proj(x, w, GDN_SEGS, (BF16, BF16, F32))
    cw = conv_w.astype(F32)
    dtb = _lane_row(dt_bias, HV_B)
    alog = _lane_row(a_log, HV_B)
    nw = norm_w.astype(F32).reshape(1, VW_B)
    o_p, s_p, conv_p = _gdn_prompt(qkv, z, ba, cw, dtb, alog, nw, batch, seq)

    def tail(a):
        return a[batch * seq:].reshape(s_batch, s_seq, a.shape[1])

    def padded(a):
        a = jnp.pad(tail(a).astype(F32), ((0, 0), (0, GDN_SAMPLE_CHUNK - s_seq), (0, 0)))
        return a.reshape(s_batch * GDN_SAMPLE_CHUNK, a.shape[2])

    xp = jnp.concatenate([conv0.astype(F32), tail(qkv).astype(F32)], axis=1)
    conv_s = xp[:, s_seq:]
    front = CONV_HALO - (CONV_W - 1)
    xp = jnp.pad(xp, ((0, 0), (front, GDN_SAMPLE_CHUNK - s_seq), (0, 0)))
    o_s, s_s = _gdn_sample(xp, padded(z), padded(ba), s_all, layer, s_carry, cw, dtb, alog, nw,
                           GDN_SAMPLE_CHUNK, s_seq)
    o_s = o_s.reshape(s_batch, GDN_SAMPLE_CHUNK, VW_B)[:, :s_seq].reshape(s_batch * s_seq, VW_B).astype(BF16)
    x = _outproj_ln((o_p, o_s), x, w_out.astype(BF16), ln_g.reshape(1, -1), ln_b.reshape(1, -1))
    return x, (s_p, conv_p), (s_s, conv_s)


def kernel(x_prompt, x_sample, state_mlstm_C, state_mlstm_n, state_mlstm_m, state_gdn_S, state_gdn_conv,
           a_w_in, a_gate_b, a_norm_w, a_w_out, b_w_in, b_conv_w, b_dt_bias, b_a_log, b_norm_w, b_w_out,
           mlp_w1, mlp_w2, ln1_g, ln1_b, ln2_g, ln2_b):
    batch, seq, d = x_prompt.shape
    s_batch, s_seq, _ = x_sample.shape
    n_prompt = batch * seq
    n_sample = s_batch * s_seq
    assert n_prompt % ROW_TILE == 0 and n_sample == ROW_TILE and seq % CHUNK == 0 and s_seq <= SAMPLE_CHUNK
    x = (x_prompt.reshape(n_prompt, d), x_sample.reshape(n_sample, d))
    p_a, s_a, p_b, s_b = [], [], [], []
    c_carry = jnp.zeros(state_mlstm_C.shape, F32)
    s_carry = jnp.zeros(state_gdn_S.shape, F32)
    for layer in range(DEPTH):
        j = layer // 2
        if layer % 2 == 0:
            x, p_state, s_state = _mlstm_layer(
                x, batch, seq, s_batch, s_seq, a_w_in[j], a_gate_b[j], a_norm_w[j], a_w_out[j],
                state_mlstm_C, j, c_carry, state_mlstm_n[j], state_mlstm_m[j], ln1_g[layer], ln1_b[layer])
            c_carry = s_state[0]
            p_a.append(p_state)
            s_a.append(s_state)
        else:
            x, p_state, s_state = _gdn_layer(
                x, batch, seq, s_batch, s_seq, b_w_in[j], b_conv_w[j], b_dt_bias[j], b_a_log[j],
                b_norm_w[j], b_w_out[j], state_gdn_S, j, s_carry, state_gdn_conv[j], ln1_g[layer], ln1_b[layer])
            s_carry = s_state[0]
            p_b.append(p_state)
            s_b.append(s_state)
        x = _mlp_ln(x, mlp_w1, mlp_w2, layer, ln2_g[layer].reshape(1, -1), ln2_b[layer].reshape(1, -1),
                    split_rows=n_prompt if layer == DEPTH - 1 else None)
    y_prompt = x[0].reshape(batch, seq, d)
    y_sample = x[1].reshape(s_batch, s_seq, d)

    def stack(states, i):
        return jnp.stack([s[i] for s in states])

    return (y_prompt, y_sample,
            stack(p_a, 0), stack(p_a, 1), stack(p_a, 2), stack(p_b, 0), stack(p_b, 1),
            c_carry, stack(s_a, 1), stack(s_a, 2), s_carry, stack(s_b, 1))
```
